```python
import math
import jax, jax.numpy as jnp
from jax import lax
import numpy as np

D_MODEL = 1024
BATCH = 16
SEQ = 2048
DEPTH = 1
DEC_BATCH = 32
DEC_SEQ = 1
PAST_LEN = 16384
PAGE_SIZE = 128

M_HEADS = 4
M_DK = 128
M_DV = 128
M_CHUNK = 64
A_HEADS = 4
A_DK = 64
A_DV = 128
Q_BLOCK = 128
P_HEADS = 8
P_NKEYS = 128
P_NEXP = P_NKEYS * P_NKEYS
P_DQ = 256
P_DHALF = P_DQ // 2
P_TOPK = 16
TOK_BLOCK = 128
PLE_DIM = 256
IN_WIDTHS = (M_HEADS * M_DK, M_HEADS * M_DK, M_HEADS * M_DV, M_HEADS * M_DV, M_HEADS, M_HEADS,
             A_HEADS * 2 * A_DK, A_HEADS * 2 * A_DK, A_HEADS * A_DV, D_MODEL, D_MODEL)
IN_DIM = sum(IN_WIDTHS)
F_GATE_OFF = sum(IN_WIDTHS[:5])
ALPHA = (2.0 * DEPTH) ** 0.25
BETA = (8.0 * DEPTH) ** -0.25
LN_EPS = 1e-5

kernel_name = 'hybrid_mlstm_diffattn_peer_step'


def layer_norm(x, g, b):
    xf = x.astype(jnp.float32)
    mu = jnp.mean(xf, -1, keepdims=True)
    var = jnp.mean(jnp.square(xf - mu), -1, keepdims=True)
    return ((xf - mu) * lax.rsqrt(var + LN_EPS)).astype(x.dtype) * g + b


def head_layer_norm(h, g):
    hf = h.astype(jnp.float32)
    mu = jnp.mean(hf, -1, keepdims=True)
    var = jnp.mean(jnp.square(hf - mu), -1, keepdims=True)
    return (hf - mu) * lax.rsqrt(var + LN_EPS) * g.astype(jnp.float32)


def head_rms_norm(h, g):
    hf = h.astype(jnp.float32)
    return hf * lax.rsqrt(jnp.mean(jnp.square(hf), -1, keepdims=True) + LN_EPS) * g.astype(jnp.float32)


def alibi_slopes():
    return 2.0 ** (-8.0 * (jnp.arange(A_HEADS, dtype=jnp.float32) + 1.0) / A_HEADS)


def project_in(x, w_in, b_in):
    bx, t = x.shape[:2]
    z = x @ w_in + b_in
    cuts, acc = [], 0
    for w in IN_WIDTHS[:-1]:
        acc += w
        cuts.append(acc)
    mq, mk, mv, mo, mi, mf, aq, ak, av, ga, gb = jnp.split(z, cuts, axis=-1)
    mq = mq.reshape(bx, t, M_HEADS, M_DK)
    mk = mk.reshape(bx, t, M_HEADS, M_DK) * (M_DK ** -0.5)
    mv = mv.reshape(bx, t, M_HEADS, M_DV)
    ig = mi.astype(jnp.float32)
    lf = jax.nn.log_sigmoid(mf.astype(jnp.float32))
    aq = aq.reshape(bx, t, A_HEADS, 2, A_DK)
    ak = ak.reshape(bx, t, A_HEADS, 2, A_DK)
    av = av.reshape(bx, t, A_HEADS, A_DV)
    return (mq, mk, mv, ig, lf), mo, (aq, ak, av), ga, gb


def mlstm_chunk(carry, inp):
    C0, n0, m0 = carry
    q, k, v, ig, lf = inp
    L = q.shape[1]
    b = jnp.cumsum(lf, axis=1)
    causal = jnp.tril(jnp.ones((L, L), dtype=bool))
    dmat = b[:, :, None, :] - b[:, None, :, :] + ig[:, None, :, :]
    dmat = jnp.where(causal[None, :, :, None], dmat, -jnp.inf)
    inter = b + m0[:, None, :]
    m = jnp.maximum(inter, jnp.max(dmat, axis=2))
    w = jnp.exp(dmat - m[:, :, None, :])
    a = jnp.exp(inter - m)
    qk = jnp.einsum('bthd,bshd->btsh', q, k) * w
    num = jnp.einsum('btsh,bshv->bthv', qk, v) + a[..., None] * jnp.einsum('bthd,bhdv->bthv', q, C0)
    den = jnp.sum(qk, axis=2) + a * jnp.einsum('bthd,bhd->bth', q, n0)
    h = num / jnp.maximum(jnp.abs(den), jnp.exp(-m))[..., None]
    mL = m[:, -1]
    wL = jnp.exp(b[:, -1:, :] - b + ig - mL[:, None, :])
    aL = jnp.exp(b[:, -1] + m0 - mL)
    C = aL[..., None, None] * C0 + jnp.einsum('bsh,bshd,bshv->bhdv', wL, k, v)
    n = aL[..., None] * n0 + jnp.einsum('bsh,bshd->bhd', wL, k)
    return (C, n, mL), h


def mlstm_prompt(q, k, v, ig, lf):
    b, s = q.shape[:2]
    nc = s // M_CHUNK

    def chunks(a):
        return a.reshape((b, nc, M_CHUNK) + a.shape[2:]).swapaxes(0, 1)

    init = (jnp.zeros((b, M_HEADS, M_DK, M_DV), jnp.float32),
            jnp.zeros((b, M_HEADS, M_DK), jnp.float32),
            jnp.zeros((b, M_HEADS), jnp.float32))
    state, h = lax.scan(mlstm_chunk, init, (chunks(q), chunks(k), chunks(v), chunks(ig), chunks(lf)))
    return state, h.swapaxes(0, 1).reshape(b, s, M_HEADS, M_DV)


def diff_attend(q, q_pos, segments, lam):
    slopes = alibi_slopes()
    scores = []
    for k, _, k_pos in segments:
        s = jnp.einsum('bqhcd,bkhcd->bchqk', q, k).astype(jnp.float32) * (A_DK ** -0.5)
        dist = (q_pos[:, None] - k_pos[None, :]).astype(jnp.float32)
        s = jnp.where(dist >= 0, s - slopes[:, None, None] * dist, -jnp.inf)
        scores.append(s)
    p = jax.nn.softmax(jnp.concatenate(scores, axis=-1), axis=-1)
    a = p[:, 0] - lam * p[:, 1]
    out, off = 0.0, 0
    for k, v, _ in segments:
        n = k.shape[1]
        out = out + jnp.einsum('bhqk,bkhv->bqhv', a[..., off:off + n].astype(v.dtype), v)
        off += n
    return out


def diff_prompt(q, k, v, lam):
    b, s = q.shape[:2]
    nqb = s // Q_BLOCK
    k_pos = jnp.arange(s, dtype=jnp.int32)
    qb = q.reshape(b, nqb, Q_BLOCK, A_HEADS, 2, A_DK).swapaxes(0, 1)

    def block(args):
        q_blk, i = args
        q_pos = i * Q_BLOCK + jnp.arange(Q_BLOCK, dtype=jnp.int32)
        return diff_attend(q_blk, q_pos, ((k, v, k_pos),), lam)

    o = lax.map(block, (qb, jnp.arange(nqb, dtype=jnp.int32)))
    return o.swapaxes(0, 1).reshape(b, s, A_HEADS, A_DV)


def diff_sample(q, k, v, k_past, v_past, lam):
    ds = q.shape[1]
    past = k_past.shape[1]
    q_pos = past + jnp.arange(ds, dtype=jnp.int32)
    segs = ((k_past, v_past, jnp.arange(past, dtype=jnp.int32)), (k, v, q_pos))
    return diff_attend(q, q_pos, segs, lam)


def merge_branches(x, hA, mo, hB, ga, gb, lam_init, mlstm_g, diff_g, w_a, w_b, w_o, g1, b1):
    bx, t = x.shape[:2]
    o_gate = jax.nn.sigmoid(mo.astype(jnp.float32)).reshape(bx, t, M_HEADS, M_DV)
    ya = (head_layer_norm(hA, mlstm_g) * o_gate).astype(x.dtype).reshape(bx, t, M_HEADS * M_DV)
    yb = (head_rms_norm(hB, diff_g) * (1.0 - lam_init)).astype(x.dtype).reshape(bx, t, A_HEADS * A_DV)
    y = (jax.nn.sigmoid(ga) * (ya @ w_a) + jax.nn.sigmoid(gb) * (yb @ w_b)) @ w_o
    return layer_norm(ALPHA * x + y, g1, b1)


def peer(x, wq, subkeys, u_tab, v_tab):
    T = x.shape[0]
    blk = min(TOK_BLOCK, T)
    nb = -(-T // blk)
    xp = jnp.pad(x, ((0, nb * blk - T), (0, 0)))

    def one(xb):
        q = (xb @ wq).reshape(blk, P_HEADS, 2, P_DHALF)
        s = jnp.einsum('thcd,hcnd->thcn', q, subkeys).astype(jnp.float32)
        sv, si = lax.top_k(s, P_TOPK)
        cand = sv[:, :, 0, :, None] + sv[:, :, 1, None, :]
        cidx = si[:, :, 0, :, None] * P_NKEYS + si[:, :, 1, None, :]
        cv, ci = lax.top_k(cand.reshape(blk, P_HEADS, P_TOPK * P_TOPK), P_TOPK)
        eidx = jnp.take_along_axis(cidx.reshape(blk, P_HEADS, P_TOPK * P_TOPK), ci, axis=-1)
        g = jax.nn.softmax(cv, axis=-1)
        u = u_tab[eidx]
        act = jax.nn.gelu(jnp.einsum('thkd,td->thk', u, xb), approximate=False)
        return jnp.einsum('thk,thkd->td', (g * act).astype(xb.dtype), v_tab[eidx])

    y = lax.map(one, xp.reshape(nb, blk, x.shape[1]))
    return y.reshape(nb * blk, x.shape[1])[:T]


def channel_sublayer(x, p, wq, subkeys, u_tab, v_tab, ple_g, ple_p, g2, b2):
    bx, t, d = x.shape
    f = peer(x.reshape(bx * t, d), wq, subkeys, u_tab, v_tab).reshape(bx, t, d)
    pe = jax.nn.sigmoid(x @ ple_g) * (p @ ple_p)
    return layer_norm(ALPHA * x + f + pe, g2, b2)


def setup_inputs(seed: int = 0) -> dict:
    key = jax.random.key(seed)
    ks = jax.random.split(key, 40)
    f32 = jnp.float32
    n_pages = PAST_LEN // PAGE_SIZE
    n_used = DEC_BATCH * n_pages
    n_pool = n_used + n_used // 4

    def nrm(k, shape, scale):
        return jax.random.normal(k, shape, f32) * scale

    page_table = jax.random.permutation(ks[0], n_pool)[:n_used].reshape(DEC_BATCH, n_pages).astype(jnp.int32)
    b_in = nrm(ks[11], (DEPTH, IN_DIM), 0.02)
    b_in = b_in.at[:, F_GATE_OFF:F_GATE_OFF + M_HEADS].add(jnp.linspace(3.0, 6.0, M_HEADS, dtype=f32))
    return {
        'x_prompt': nrm(ks[1], (BATCH, SEQ, D_MODEL), 1.0),
        'x_sample': nrm(ks[2], (DEC_BATCH, DEC_SEQ, D_MODEL), 1.0),
        'cache_k': nrm(ks[3], (DEPTH, n_pool, PAGE_SIZE, A_HEADS, 2, A_DK), 1.0),
        'cache_v': nrm(ks[4], (DEPTH, n_pool, PAGE_SIZE, A_HEADS, A_DV), 1.0),
        'state_C': nrm(ks[5], (DEPTH, DEC_BATCH, M_HEADS, M_DK, M_DV), 0.1),
        'state_n': nrm(ks[6], (DEPTH, DEC_BATCH, M_HEADS, M_DK), 0.1),
        'state_m': nrm(ks[7], (DEPTH, DEC_BATCH, M_HEADS), 1.0),
        'page_table': page_table,
        'p_prompt': nrm(ks[8], (DEPTH, BATCH, SEQ, PLE_DIM), 1.0),
        'p_sample': nrm(ks[9], (DEPTH, DEC_BATCH, DEC_SEQ, PLE_DIM), 1.0),
        'w_in': nrm(ks[10], (DEPTH, D_MODEL, IN_DIM), D_MODEL ** -0.5),
        'b_in': b_in,
        'lambda_q1': nrm(ks[12], (DEPTH, A_DK), 0.1),
        'lambda_k1': nrm(ks[13], (DEPTH, A_DK), 0.1),
        'lambda_q2': nrm(ks[14], (DEPTH, A_DK), 0.1),
        'lambda_k2': nrm(ks[15], (DEPTH, A_DK), 0.1),
        'mlstm_norm_g': 1.0 + nrm(ks[16], (DEPTH, M_HEADS, M_DV), 0.02),
        'diff_norm_g': 1.0 + nrm(ks[17], (DEPTH, A_HEADS, A_DV), 0.02),
        'w_branch_a': nrm(ks[18], (DEPTH, M_HEADS * M_DV, D_MODEL), (M_HEADS * M_DV) ** -0.5),
        'w_branch_b': nrm(ks[19], (DEPTH, A_HEADS * A_DV, D_MODEL), (A_HEADS * A_DV) ** -0.5),
        'w_out': nrm(ks[20], (DEPTH, D_MODEL, D_MODEL), BETA * D_MODEL ** -0.5),
        'ln1_g': 1.0 + nrm(ks[21], (DEPTH, D_MODEL), 0.02),
        'ln1_b': nrm(ks[22], (DEPTH, D_MODEL), 0.02),
        'peer_wq': nrm(ks[23], (DEPTH, D_MODEL, P_HEADS * P_DQ), D_MODEL ** -0.5),
        'peer_subkeys': nrm(ks[24], (DEPTH, P_HEADS, 2, P_NKEYS, P_DHALF), P_DHALF ** -0.5),
        'peer_u': nrm(ks[25], (DEPTH, P_NEXP, D_MODEL), D_MODEL ** -0.5),
        'peer_v': nrm(ks[26], (DEPTH, P_NEXP, D_MODEL), BETA * P_HEADS ** -0.5),
        'ple_w_gate': nrm(ks[27], (DEPTH, D_MODEL, D_MODEL), D_MODEL ** -0.5),
        'ple_w_proj': nrm(ks[28], (DEPTH, PLE_DIM, D_MODEL), BETA * PLE_DIM ** -0.5),
        'ln2_g': 1.0 + nrm(ks[29], (DEPTH, D_MODEL), 0.02),
        'ln2_b': nrm(ks[30], (DEPTH, D_MODEL), 0.02),
    }


def reference(x_prompt, x_sample, cache_k, cache_v, state_C, state_n, state_m, page_table,
              p_prompt, p_sample, w_in, b_in, lambda_q1, lambda_k1, lambda_q2, lambda_k2,
              mlstm_norm_g, diff_norm_g, w_branch_a, w_branch_b, w_out, ln1_g, ln1_b,
              peer_wq, peer_subkeys, peer_u, peer_v, ple_w_gate, ple_w_proj, ln2_g, ln2_b):
    f32 = jnp.float32
    db = x_sample.shape[0]
    past = page_table.shape[1] * PAGE_SIZE
    hp, hs = x_prompt, x_sample
    kp_l, vp_l, cp_l, np_l, mp_l = [], [], [], [], []
    ks_l, vs_l, cs_l, ns_l, ms_l = [], [], [], [], []
    for l in range(DEPTH):
        lam_init = 0.8 - 0.6 * math.exp(-0.3 * l)
        lam = (jnp.exp(jnp.sum(lambda_q1[l].astype(f32) * lambda_k1[l].astype(f32)))
               - jnp.exp(jnp.sum(lambda_q2[l].astype(f32) * lambda_k2[l].astype(f32))) + lam_init)
        (mq, mk, mv, ig, lf), mo, (aq, ak, av), ga, gb = project_in(hp, w_in[l], b_in[l])
        (C, n, m), hA = mlstm_prompt(mq, mk, mv, ig, lf)
        hB = diff_prompt(aq, ak, av, lam)
        x1 = merge_branches(hp, hA, mo, hB, ga, gb, lam_init, mlstm_norm_g[l], diff_norm_g[l],
                            w_branch_a[l], w_branch_b[l], w_out[l], ln1_g[l], ln1_b[l])
        hp = channel_sublayer(x1, p_prompt[l], peer_wq[l], peer_subkeys[l], peer_u[l], peer_v[l],
                              ple_w_gate[l], ple_w_proj[l], ln2_g[l], ln2_b[l])
        kp_l.append(ak); vp_l.append(av); cp_l.append(C); np_l.append(n); mp_l.append(m)
        (mq, mk, mv, ig, lf), mo, (aq, ak, av), ga, gb = project_in(hs, w_in[l], b_in[l])
        carry0 = (state_C[l].astype(f32), state_n[l].astype(f32), state_m[l].astype(f32))
        (C, n, m), hA = mlstm_chunk(carry0, (mq, mk, mv, ig, lf))
        k_past = cache_k[l, page_table].reshape(db, past, A_HEADS, 2, A_DK)
        v_past = cache_v[l, page_table].reshape(db, past, A_HEADS, A_DV)
        hB = diff_sample(aq, ak, av, k_past, v_past, lam)
        x1 = merge_branches(hs, hA, mo, hB, ga, gb, lam_init, mlstm_norm_g[l], diff_norm_g[l],
                            w_branch_a[l], w_branch_b[l], w_out[l], ln1_g[l], ln1_b[l])
        hs = channel_sublayer(x1, p_sample[l], peer_wq[l], peer_subkeys[l], peer_u[l], peer_v[l],
                              ple_w_gate[l], ple_w_proj[l], ln2_g[l], ln2_b[l])
        ks_l.append(ak); vs_l.append(av); cs_l.append(C); ns_l.append(n); ms_l.append(m)
    k_prompt = jnp.stack(kp_l, 0)
    v_prompt = jnp.stack(vp_l, 0)
    C_prompt = jnp.stack(cp_l, 0)
    n_prompt = jnp.stack(np_l, 0)
    m_prompt = jnp.stack(mp_l, 0)
    k_sample = jnp.stack(ks_l, 0)
    v_sample = jnp.stack(vs_l, 0)
    C_sample = jnp.stack(cs_l, 0)
    n_sample = jnp.stack(ns_l, 0)
    m_sample = jnp.stack(ms_l, 0)
    return (hp, hs, k_prompt, v_prompt, C_prompt, n_prompt, m_prompt,
            k_sample, v_sample, C_sample, n_sample, m_sample)
```

```python
import functools
import math

import jax
import jax.numpy as jnp
from jax import lax
from jax.experimental import pallas as pl
from jax.experimental.pallas import tpu as pltpu

F32 = jnp.float32
BF16 = jnp.bfloat16
I32 = jnp.int32

M_HEADS, M_DK, M_DV = 4, 128, 128
A_HEADS, A_DK, A_DV = 4, 64, 128
P_HEADS, P_NKEYS, P_DHALF, P_TOPK = 8, 128, 128, 16
PAGE_SIZE = 128
LN_EPS = 1e-5
N_SEL = P_HEADS * P_TOPK

V7X_LANES = 128
V7X_SUBLANES = 8
V7X_VMEM_LIMIT = 56 * 1024 * 1024

MLSTM_CHUNK = 128
ATTN_BLOCK = 256
PAGES_PER_STEP = 8
ROW_TILE = 256
PEER_TILE = 128

NEG_INF = float("-inf")


def _cparams(*sem):
    return pltpu.CompilerParams(dimension_semantics=sem, vmem_limit_bytes=V7X_VMEM_LIMIT)


def _nt_dot(a, b):
    return lax.dot_general(a, b, (((1,), (1,)), ((), ())), preferred_element_type=F32)


def _dot(a, b):
    return jnp.dot(a, b, preferred_element_type=F32)


_C_MQ, _C_MK, _C_MV, _C_MO, _C_AQ, _C_AK, _C_AV, _C_GA = 0, 512, 1024, 1536, 2048, 2560, 3072, 3584
_C_GB, _C_GATE, _C_END = 4608, 5632, 5760


def _proj_kernel(x_ref, w_ref, b_ref, mq, mk, mv, so, aq, akf, avf, akb, avb, sga, sgb, gates):
    xb = x_ref[...].astype(BF16)

    def seg(lo, hi):
        return _dot(xb, w_ref[:, lo:hi]) + b_ref[:, lo:hi]

    mq[...] = seg(_C_MQ, _C_MK).astype(BF16)
    mk[...] = (seg(_C_MK, _C_MV) * (M_DK ** -0.5)).astype(BF16)
    mv[...] = seg(_C_MV, _C_MO).astype(BF16)
    so[...] = jax.nn.sigmoid(seg(_C_MO, _C_AQ)).astype(BF16)
    aq[...] = (seg(_C_AQ, _C_AK) * (A_DK ** -0.5)).astype(BF16)
    k = seg(_C_AK, _C_AV)
    akf[...] = k
    akb[...] = k.astype(BF16)
    v = seg(_C_AV, _C_GA)
    avf[...] = v
    avb[...] = v.astype(BF16)
    sga[...] = jax.nn.sigmoid(seg(_C_GA, _C_GB)).astype(BF16)
    sgb[...] = jax.nn.sigmoid(seg(_C_GB, _C_GATE)).astype(BF16)
    gates[...] = seg(_C_GATE, _C_END)


def _project(x2, w, b):
    t, d = x2.shape
    tm = min(ROW_TILE, t)
    row = lambda i: (i, 0)
    fixed = lambda i: (0, 0)
    widths = [(512, BF16)] * 5 + [(512, F32)] * 2 + [(512, BF16)] * 2 + [(1024, BF16)] * 2 + [(128, F32)]
    return pl.pallas_call(
        _proj_kernel,
        grid=(t // tm,),
        in_specs=[pl.BlockSpec((tm, d), row), pl.BlockSpec(w.shape, fixed), pl.BlockSpec(b.shape, fixed)],
        out_specs=[pl.BlockSpec((tm, n), row) for n, _ in widths],
        out_shape=[jax.ShapeDtypeStruct((t, n), dt) for n, dt in widths],
        compiler_params=_cparams("parallel"),
        name="project_in",
    )(x2, w, b)


def _head_ln_gate(hh, g_row, so):
    mu = jnp.mean(hh, axis=-1, keepdims=True)
    var = jnp.mean(jnp.square(hh - mu), axis=-1, keepdims=True)
    return ((hh - mu) * lax.rsqrt(var + LN_EPS) * g_row * so.astype(F32)).astype(BF16)


def _mlstm_prompt_kernel(q_ref, k_ref, v_ref, so_ref, g_ref, ng_ref, ya_ref, c_ref, n_ref, m_ref):
    @pl.when(pl.program_id(1) == 0)
    def _():
        c_ref[...] = jnp.zeros_like(c_ref)
        n_ref[...] = jnp.zeros_like(n_ref)
        m_ref[...] = jnp.zeros_like(m_ref)

    L = q_ref.shape[0]
    gate = g_ref[...]
    cum = jax.nn.log_sigmoid(gate)
    row = lax.broadcasted_iota(I32, cum.shape, 0)
    sh = 1
    while sh < L:
        cum = cum + jnp.where(row >= sh, pltpu.roll(cum, sh, axis=0), 0.0)
        sh *= 2
    cum_t = cum.T
    gate_t = gate.T
    causal = lax.broadcasted_iota(I32, (L, L), 1) <= lax.broadcasted_iota(I32, (L, L), 0)
    lane = lax.broadcasted_iota(I32, (1, V7X_LANES), 1)
    m_all = m_ref[0]
    m_next = m_all
    for h in range(M_HEADS):
        hs = slice(h * M_DK, (h + 1) * M_DK)
        b_col = cum[:, M_HEADS + h:M_HEADS + h + 1]
        b_row = cum_t[M_HEADS + h:M_HEADS + h + 1, :]
        i_col = gate[:, h:h + 1]
        i_row = gate_t[h:h + 1, :]
        m0 = m_all[:, h:h + 1]
        dmat = jnp.where(causal, b_col - b_row + i_row, NEG_INF)
        inter = b_col + m0
        m = jnp.maximum(inter, jnp.max(dmat, axis=1, keepdims=True))
        w = jnp.exp(dmat - m)
        a = jnp.exp(inter - m)
        qh, kh, vh = q_ref[:, hs], k_ref[:, hs], v_ref[:, hs]
        c0 = c_ref[0, h]
        n0 = n_ref[0, h:h + 1, :]
        qk = _nt_dot(qh, kh) * w
        num = _dot(qk.astype(BF16), vh) + a * _dot(qh, c0.astype(BF16))
        den = jnp.sum(qk, axis=1, keepdims=True) + a * jnp.sum(qh.astype(F32) * n0, axis=1, keepdims=True)
        hh = num / jnp.maximum(jnp.abs(den), jnp.exp(-m))
        ya_ref[:, hs] = _head_ln_gate(hh, ng_ref[:, hs], so_ref[:, hs])
        m_last = m[L - 1:L, :]
        b_last = b_col[L - 1:L, :]
        w_last = jnp.exp(b_last - b_col + i_col - m_last)
        a_last = jnp.exp(b_last + m0 - m_last)
        kw = kh.astype(F32) * w_last
        c_ref[0, h] = a_last * c0 + _dot(kw.T.astype(BF16), vh)
        n_ref[0, h:h + 1, :] = a_last * n0 + jnp.sum(kw, axis=0, keepdims=True)
        m_next = jnp.where(lane == h, m_last, m_next)
    m_ref[0] = m_next


def _mlstm_prompt(mq, mk, mv, so, gates, ng, batch, seq):
    L = min(MLSTM_CHUNK, seq)
    nc = seq // L
    tok = lambda b, c: (b * nc + c, 0)
    wide = pl.BlockSpec((L, M_HEADS * M_DK), tok)
    return pl.pallas_call(
        _mlstm_prompt_kernel,
        grid=(batch, nc),
        in_specs=[wide, wide, wide, wide, pl.BlockSpec((L, V7X_LANES), tok),
                  pl.BlockSpec((1, M_HEADS * M_DV), lambda b, c: (0, 0))],
        out_specs=[wide,
                   pl.BlockSpec((1, M_HEADS, M_DK, M_DV), lambda b, c: (b, 0, 0, 0)),
                   pl.BlockSpec((1, M_HEADS, M_DK), lambda b, c: (b, 0, 0)),
                   pl.BlockSpec((1, 1, V7X_LANES), lambda b, c: (b, 0, 0))],
        out_shape=[jax.ShapeDtypeStruct((batch * seq, M_HEADS * M_DV), BF16),
                   jax.ShapeDtypeStruct((batch, M_HEADS, M_DK, M_DV), F32),
                   jax.ShapeDtypeStruct((batch, M_HEADS, M_DK), F32),
                   jax.ShapeDtypeStruct((batch, 1, V7X_LANES), F32)],
        compiler_params=_cparams("parallel", "arbitrary"),
        name="mlstm_prompt",
    )(mq, mk, mv, so, gates, ng)


def _mlstm_step_kernel(q_ref, k_ref, v_ref, so_ref, g_ref, ng_ref, c0_ref, n0_ref, m0_ref,
                       ya_ref, c_ref, n_ref, m_ref):
    gate = g_ref[0]
    m_all = m0_ref[0]
    lane = lax.broadcasted_iota(I32, (1, V7X_LANES), 1)
    diag = lax.broadcasted_iota(I32, (M_DK, M_DK), 0) == lax.broadcasted_iota(I32, (M_DK, M_DK), 1)
    m_next = m_all
    for h in range(M_HEADS):
        hs = slice(h * M_DK, (h + 1) * M_DK)
        q = q_ref[0][:, hs].astype(F32)
        k = k_ref[0][:, hs].astype(F32)
        v = v_ref[0][:, hs].astype(F32)
        ig = gate[:, h:h + 1]
        lf = jax.nn.log_sigmoid(gate[:, M_HEADS + h:M_HEADS + h + 1])
        m0 = m_all[:, h:h + 1]
        inter = lf + m0
        m = jnp.maximum(inter, ig)
        w = jnp.exp(ig - m)
        a = jnp.exp(inter - m)
        c0 = c0_ref[0, h]
        n0 = n0_ref[0, h:h + 1, :]
        qk = jnp.sum(q * k, axis=1, keepdims=True) * w
        q8 = jnp.broadcast_to(q, (V7X_SUBLANES, M_DK)).astype(BF16)
        qc = _dot(q8, c0.astype(BF16))[0:1, :]
        num = qk * v + a * qc
        den = qk + a * jnp.sum(q * n0, axis=1, keepdims=True)
        hh = num / jnp.maximum(jnp.abs(den), jnp.exp(-m))
        ya_ref[0, :, hs] = _head_ln_gate(hh, ng_ref[:, hs], so_ref[0][:, hs])
        kd = jnp.where(diag, jnp.broadcast_to(k, (M_DK, M_DK)), 0.0).astype(BF16)
        vb = jnp.broadcast_to(v, (M_DK, M_DV)).astype(BF16)
        c_ref[0, h] = a * c0 + w * _dot(kd, vb)
        n_ref[0, h:h + 1, :] = a * n0 + w * k
        m_next = jnp.where(lane == h, m, m_next)
    m_ref[0] = m_next


def _mlstm_step(mq, mk, mv, so, gates, ng, c0, n0, m0):
    db = mq.shape[0]
    r3 = lambda a: a.reshape(db, 1, a.shape[-1])
    tok = lambda b: (b, 0, 0)
    wide = pl.BlockSpec((1, 1, M_HEADS * M_DK), tok)
    narrow = pl.BlockSpec((1, 1, V7X_LANES), tok)
    cspec = pl.BlockSpec((1, M_HEADS, M_DK, M_DV), lambda b: (b, 0, 0, 0))
    nspec = pl.BlockSpec((1, M_HEADS, M_DK), tok)
    return pl.pallas_call(
        _mlstm_step_kernel,
        grid=(db,),
        in_specs=[wide, wide, wide, wide, narrow, pl.BlockSpec((1, M_HEADS * M_DV), lambda b: (0, 0)),
                  cspec, nspec, narrow],
        out_specs=[wide, cspec, nspec, narrow],
        out_shape=[jax.ShapeDtypeStruct((db, 1, M_HEADS * M_DV), BF16),
                   jax.ShapeDtypeStruct(c0.shape, F32),
                   jax.ShapeDtypeStruct(n0.shape, F32),
                   jax.ShapeDtypeStruct((db, 1, V7X_LANES), F32)],
        compiler_params=_cparams("parallel"),
        name="mlstm_step",
    )(r3(mq), r3(mk), r3(mv), r3(so), r3(gates), ng, c0, n0, m0)


def _lambda_value(lq1, lk1, lq2, lk2, lam_init):
    s1 = jnp.sum(lq1[...] * lk1[...], axis=1, keepdims=True)
    s2 = jnp.sum(lq2[...] * lk2[...], axis=1, keepdims=True)
    return jnp.exp(s1) - jnp.exp(s2) + lam_init


def _rms_gain(o, g_row, lam_init):
    return o * lax.rsqrt(jnp.mean(jnp.square(o), axis=-1, keepdims=True) + LN_EPS) * g_row * (1.0 - lam_init)


def _attn_prompt_kernel(q_ref, k_ref, v_ref, slope_ref, ng_ref, lq1, lk1, lq2, lk2, o_ref, *, lam_init):
    tq = q_ref.shape[0]
    qi = pl.program_id(2)
    q = q_ref[...]
    lane = lax.broadcasted_iota(I32, q.shape, 1)
    zero = jnp.zeros_like(q)
    qq = jnp.concatenate([jnp.where(lane < A_DK, q, zero), jnp.where(lane >= A_DK, q, zero)], axis=0)
    slope = slope_ref[0]
    slope_row = jnp.concatenate([slope] * (tq // V7X_LANES), axis=1)
    col = lax.broadcasted_iota(I32, (1, tq), 1)

    def block(j, carry, masked):
        m, l, acc = carry
        start = pl.multiple_of(j * tq, tq)
        kb = k_ref[pl.ds(start, tq), :]
        vb = v_ref[pl.ds(start, tq), :]
        s = _nt_dot(qq, kb) + slope_row * (j * tq + col).astype(F32)
        if masked:
            r = lax.broadcasted_iota(I32, s.shape, 0)
            r = jnp.where(r >= tq, r - tq, r)
            s = jnp.where(lax.broadcasted_iota(I32, s.shape, 1) <= r, s, NEG_INF)
        m_new = jnp.maximum(m, jnp.max(s, axis=1, keepdims=True))
        p = jnp.exp(s - m_new)
        alpha = jnp.exp(m - m_new)
        l = alpha * l + jnp.sum(p, axis=1, keepdims=True)
        acc = alpha * acc + _dot(p.astype(BF16), vb)
        return m_new, l, acc

    init = (jnp.full((2 * tq, 1), NEG_INF, F32), jnp.zeros((2 * tq, 1), F32), jnp.zeros((2 * tq, A_DV), F32))
    carry = lax.fori_loop(0, qi, lambda j, c: block(j, c, False), init)
    _, l, acc = block(qi, carry, True)
    lam = _lambda_value(lq1, lk1, lq2, lk2, lam_init)
    o = acc[:tq] / l[:tq] - lam * (acc[tq:] / l[tq:])
    o_ref[...] = _rms_gain(o, ng_ref[...], lam_init).astype(BF16)


def _attn_prompt(aq, akb, avb, slopes, ng, lams, batch, seq, lam_init):
    tq = min(ATTN_BLOCK, seq)
    nq = seq // tq
    hw = 2 * A_DK
    lam_spec = pl.BlockSpec((1, A_DK), lambda b, h, i: (0, 0))
    return pl.pallas_call(
        functools.partial(_attn_prompt_kernel, lam_init=lam_init),
        grid=(batch, A_HEADS, nq),
        in_specs=[pl.BlockSpec((tq, hw), lambda b, h, i: (b * nq + i, h)),
                  pl.BlockSpec((seq, hw), lambda b, h, i: (b, h)),
                  pl.BlockSpec((seq, A_DV), lambda b, h, i: (b, h)),
                  pl.BlockSpec((1, 1, V7X_LANES), lambda b, h, i: (h, 0, 0)),
                  pl.BlockSpec((1, A_DV), lambda b, h, i: (0, h)),
                  lam_spec, lam_spec, lam_spec, lam_spec],
        out_specs=pl.BlockSpec((tq, A_DV), lambda b, h, i: (b * nq + i, h)),
        out_shape=jax.ShapeDtypeStruct((batch * seq, A_HEADS * A_DV), BF16),
        compiler_params=_cparams("parallel", "parallel", "arbitrary"),
        name="diff_attn_prompt",
    )(aq, akb, avb, slopes, ng, *lams)


def _attn_decode_kernel(pt_ref, q_ref, ks_ref, vs_ref, slope_ref, ng_ref, lq1, lk1, lq2, lk2, *rest,
                        lam_init, pps):
    k_refs, v_refs = rest[:pps], rest[pps:2 * pps]
    o_ref, m_ref, l_ref, acc_ref = rest[2 * pps:]
    j = pl.program_id(1)
    nmap = 2 * A_HEADS
    width = A_HEADS * 2 * A_DK

    @pl.when(j == 0)
    def _():
        m_ref[...] = jnp.full(m_ref.shape, NEG_INF, F32)
        l_ref[...] = jnp.zeros_like(l_ref)
        acc_ref[...] = jnp.zeros_like(acc_ref)

    r8 = lax.broadcasted_iota(I32, (nmap, width), 0)
    c8 = lax.broadcasted_iota(I32, (nmap, width), 1)
    own = (c8 // A_DK) == r8
    qmat = jnp.where(own, jnp.broadcast_to(q_ref[0].astype(F32), (nmap, width)), 0.0)
    qb = qmat.astype(BF16)
    slope = slope_ref[...]
    lane = lax.broadcasted_iota(I32, (nmap, PAGE_SIZE), 1)

    def fold(s, v_rows):
        m_old = m_ref[...]
        m_new = jnp.maximum(m_old, jnp.max(s, axis=1, keepdims=True))
        alpha = jnp.exp(m_old - m_new)
        p = jnp.exp(s - m_new[:, 0:s.shape[1]])
        m_ref[...] = m_new
        return alpha, p

    for i in range(pps):
        kp = k_refs[i][0].astype(BF16)
        vp = v_refs[i][0].astype(BF16)
        pos = ((j * pps + i) * PAGE_SIZE + lane).astype(F32)
        s = _nt_dot(qb, kp) + slope * pos
        alpha, p = fold(s, vp)
        l_ref[...] = alpha * l_ref[...] + jnp.sum(p, axis=1, keepdims=True)
        acc_ref[...] = alpha[:, 0:1] * acc_ref[...] + _dot(p.astype(BF16), vp)

    @pl.when(j == pl.num_programs(1) - 1)
    def _():
        past = jnp.asarray(pl.num_programs(1) * pps * PAGE_SIZE, F32)
        k_new = ks_ref[0].astype(F32)
        v_new = vs_ref[0].astype(F32)
        s = jnp.sum(qmat * k_new, axis=1, keepdims=True) + slope[:, 0:1] * past
        alpha, p = fold(s, v_new)
        l = alpha * l_ref[...] + p
        acc = alpha[:, 0:1] * acc_ref[...] + p * v_new
        o = acc / l[:, 0:1]
        lam = _lambda_value(lq1, lk1, lq2, lk2, lam_init)
        d = o - lam * pltpu.roll(o, nmap - 1, axis=0)
        keep = r8 == 2 * (c8 // A_DV)
        hb = jnp.sum(jnp.where(keep, d, 0.0), axis=0, keepdims=True)
        for h in range(A_HEADS):
            hs = slice(h * A_DV, (h + 1) * A_DV)
            o_ref[0, :, hs] = _rms_gain(hb[:, hs], ng_ref[:, hs], lam_init).astype(BF16)


def _attn_decode(page_table, aq, akb, avb, cache_k, cache_v, slopes8, ng, lams, lam_init):
    db, n_pages = page_table.shape
    pps = math.gcd(PAGES_PER_STEP, n_pages)
    width = A_HEADS * 2 * A_DK
    n_pool = cache_k.shape[0]
    ck = cache_k.reshape(n_pool, PAGE_SIZE, width)
    cv = cache_v.reshape(n_pool, PAGE_SIZE, A_HEADS * A_DV)
    r3 = lambda a: a.reshape(db, 1, a.shape[-1])
    tok = lambda b, j, pt: (b, 0, 0)
    fixed = lambda b, j, pt: (0, 0)
    page_specs = [pl.BlockSpec((1, PAGE_SIZE, width), lambda b, j, pt, i=i: (pt[b, j * pps + i], 0, 0))
                  for i in range(pps)]
    lam_spec = pl.BlockSpec((1, A_DK), fixed)
    grid_spec = pltpu.PrefetchScalarGridSpec(
        num_scalar_prefetch=1,
        grid=(db, n_pages // pps),
        in_specs=[pl.BlockSpec((1, 1, width), tok), pl.BlockSpec((1, 1, width), tok),
                  pl.BlockSpec((1, 1, width), tok),
                  pl.BlockSpec((2 * A_HEADS, V7X_LANES), fixed), pl.BlockSpec((1, width), fixed),
                  lam_spec, lam_spec, lam_spec, lam_spec] + page_specs + page_specs,
        out_specs=pl.BlockSpec((1, 1, width), tok),
        scratch_shapes=[pltpu.VMEM((2 * A_HEADS, V7X_LANES), F32), pltpu.VMEM((2 * A_HEADS, V7X_LANES), F32),
                        pltpu.VMEM((2 * A_HEADS, width), F32)],
    )
    out = pl.pallas_call(
        functools.partial(_attn_decode_kernel, lam_init=lam_init, pps=pps),
        grid_spec=grid_spec,
        out_shape=jax.ShapeDtypeStruct((db, 1, width), BF16),
        compiler_params=_cparams("parallel", "arbitrary"),
        name="diff_attn_decode",
    )(page_table, r3(aq), r3(akb), r3(avb), slopes8, ng, *lams, *([ck] * pps), *([cv] * pps))
    return out.reshape(db, width)


def _layer_norm(x, g, b):
    mu = jnp.mean(x, axis=-1, keepdims=True)
    var = jnp.mean(jnp.square(x - mu), axis=-1, keepdims=True)
    return (x - mu) * lax.rsqrt(var + LN_EPS) * g + b


def _merge_kernel(x_ref, ya_ref, yb_ref, sga_ref, sgb_ref, p_ref, wa_ref, wb_ref, wo_ref, g1_ref, b1_ref,
                  wg_ref, wp_ref, wq_ref, x1_ref, base_ref, q_ref, *, alpha):
    ya = _dot(ya_ref[...], wa_ref[...])
    yb = _dot(yb_ref[...], wb_ref[...])
    y = sga_ref[...].astype(F32) * ya + sgb_ref[...].astype(F32) * yb
    y = _dot(y.astype(BF16), wo_ref[...])
    x1 = _layer_norm(alpha * x_ref[...] + y, g1_ref[...], b1_ref[...])
    x1_ref[...] = x1
    x1b = x1.astype(BF16)
    pe = jax.nn.sigmoid(_dot(x1b, wg_ref[...])) * _dot(p_ref[...].astype(BF16), wp_ref[...])
    base_ref[...] = alpha * x1 + pe
    q_ref[...] = _dot(x1b, wq_ref[...]).astype(BF16)


def _merge(x2, ya, yb, sga, sgb, p2, wa, wb, wo, g1, b1, wg, wp, wq, alpha):
    t, d = x2.shape
    tm = min(ROW_TILE, t)
    row = lambda i: (i, 0)
    fixed = lambda i: (0, 0)
    rows = lambda a: pl.BlockSpec((tm, a.shape[1]), row)
    full = lambda a: pl.BlockSpec(a.shape, fixed)
    nq = wq.shape[1]
    return pl.pallas_call(
        functools.partial(_merge_kernel, alpha=alpha),
        grid=(t // tm,),
        in_specs=[rows(x2), rows(ya), rows(yb), rows(sga), rows(sgb), rows(p2),
                  full(wa), full(wb), full(wo), full(g1), full(b1), full(wg), full(wp), full(wq)],
        out_specs=[pl.BlockSpec((tm, d), row), pl.BlockSpec((tm, d), row), pl.BlockSpec((tm, nq), row)],
        out_shape=[jax.ShapeDtypeStruct((t, d), F32), jax.ShapeDtypeStruct((t, d), F32),
                   jax.ShapeDtypeStruct((t, nq), BF16)],
        compiler_params=_cparams("parallel"),
        name="merge_ln1",
    )(x2, ya, yb, sga, sgb, p2, wa, wb, wo, g1, b1, wg, wp, wq)


def _topk_rows(s, k, payload=None):
    n = s.shape[0]
    rows = lax.broadcasted_iota(I32, s.shape, 0)
    vals, idxs = [], []
    for _ in range(k):
        mx = jnp.max(s, axis=0, keepdims=True)
        idx = jnp.min(jnp.where(s == mx, rows, n), axis=0, keepdims=True)
        sel = rows == idx
        if payload is None:
            idxs.append(idx)
        else:
            idxs.append(jnp.sum(jnp.where(sel, payload, 0), axis=0, keepdims=True))
        vals.append(mx)
        s = jnp.where(sel, NEG_INF, s)
    return jnp.concatenate(vals, axis=0), jnp.concatenate(idxs, axis=0)


def _route_kernel(q_ref, sk_ref, idx_ref, g_ref, gt_ref):
    h = pl.program_id(1)
    q = q_ref[...]
    sv, si = [], []
    for c in range(2):
        s = _nt_dot(sk_ref[0, c], q[:, c * P_DHALF:(c + 1) * P_DHALF])
        v, i = _topk_rows(s, P_TOPK)
        sv.append(v)
        si.append(i)
    cand = jnp.concatenate([sv[0][a:a + 1] + sv[1] for a in range(P_TOPK)], axis=0)
    cidx = jnp.concatenate([si[0][a:a + 1] * P_NKEYS + si[1] for a in range(P_TOPK)], axis=0)
    cv, eidx = _topk_rows(cand, P_TOPK, payload=cidx)
    e = jnp.exp(cv - cv[0:1])
    g = e / jnp.sum(e, axis=0, keepdims=True)
    idx_ref[0] = eidx
    gt_ref[pl.ds(pl.multiple_of(h * P_TOPK, P_TOPK), P_TOPK), :] = g

    @pl.when(h == P_HEADS - 1)
    def _():
        g_ref[...] = gt_ref[...].T


def _route(q, subkeys, tm):
    t = q.shape[0]
    nblk = t // tm
    return pl.pallas_call(
        _route_kernel,
        grid=(nblk, P_HEADS),
        in_specs=[pl.BlockSpec((tm, 2 * P_DHALF), lambda i, h: (i, h)),
                  pl.BlockSpec((1, 2, P_NKEYS, P_DHALF), lambda i, h: (h, 0, 0, 0))],
        out_specs=[pl.BlockSpec((1, P_TOPK, tm), lambda i, h: (i, h, 0)),
                   pl.BlockSpec((tm, N_SEL), lambda i, h: (i, 0))],
        out_shape=[jax.ShapeDtypeStruct((nblk, N_SEL, tm), I32),
                   jax.ShapeDtypeStruct((t, N_SEL), F32)],
        scratch_shapes=[pltpu.VMEM((N_SEL, tm), F32)],
        compiler_params=_cparams("parallel", "arbitrary"),
        name="peer_route",
    )(q, subkeys)


def _load_row(tab_ref, e):
    return tab_ref[e].astype(F32)


def _sublane_merge(x, y, sh, mask):
    c = jnp.where(mask, x, y)
    d = jnp.where(mask, y, x)
    if 2 * sh == V7X_SUBLANES:
        return c + pltpu.roll(d, sh, axis=0)
    return c + jnp.where(mask, pltpu.roll(d, V7X_SUBLANES - sh, axis=0), pltpu.roll(d, sh, axis=0))


def _stage_blocks(i, nblk, tab_hbm, tab_vmem, per_block, sems):
    slot = i % 2

    def copies(blk, sl):
        return [pltpu.make_async_copy(h.at[blk], s.at[sl], sems.at[1 + 2 * n + sl])
                for n, (h, s) in enumerate(per_block)]

    @pl.when(i == 0)
    def _():
        table = pltpu.make_async_copy(tab_hbm, tab_vmem, sems.at[0])
        table.start()
        for c in copies(0, 0):
            c.start()
        table.wait()

    for c in copies(i, slot):
        c.wait()

    @pl.when(i + 1 < nblk)
    def _():
        for c in copies(i + 1, 1 - slot):
            c.start()

    return slot


def _peer_up_kernel(x_ref, g_ref, idx_hbm, tab_hbm, w_ref, tab_vmem, idx_smem, s_ref, sems):
    i = pl.program_id(0)
    slot = _stage_blocks(i, pl.num_programs(0), tab_hbm, tab_vmem, [(idx_hbm, idx_smem)], sems)
    tb = x_ref.shape[0]
    sub = lax.broadcasted_iota(I32, (V7X_SUBLANES, V7X_LANES), 0)
    masks = {sh: (sub & sh) == 0 for sh in (4, 2, 1)}
    ones = jnp.ones((V7X_LANES, V7X_LANES), F32)
    diag = lax.broadcasted_iota(I32, (N_SEL, V7X_LANES), 0) == lax.broadcasted_iota(I32, (N_SEL, V7X_LANES), 1)

    def token(t, rows8):
        xt = x_ref[t]
        groups = []
        for j in range(N_SEL // V7X_SUBLANES):
            parts = [_load_row(tab_vmem, idx_smem[slot, j * V7X_SUBLANES + r, t]) * xt
                     for r in range(V7X_SUBLANES)]
            for sh in (4, 2, 1):
                parts = [_sublane_merge(parts[a], parts[a + sh], sh, masks[sh])
                         for a in range(len(parts)) if (a & sh) == 0]
            groups.append(parts[0])
        part = jnp.concatenate(groups, axis=0)
        tot = jnp.dot(part, ones, preferred_element_type=F32, precision=lax.Precision.HIGHEST)
        srow = jnp.sum(jnp.where(diag, tot, 0.0), axis=0, keepdims=True)
        rows8 = jnp.where(sub == t % V7X_SUBLANES, srow, rows8)
        s_ref[pl.ds(pl.multiple_of((t // V7X_SUBLANES) * V7X_SUBLANES, V7X_SUBLANES), V7X_SUBLANES), :] = rows8
        return rows8

    lax.fori_loop(0, tb, token, jnp.zeros((V7X_SUBLANES, V7X_LANES), F32))
    s = s_ref[...]
    act = 0.5 * s * (1.0 + lax.erf(s * (2.0 ** -0.5)))
    w_ref[...] = g_ref[...] * act


def _peer_down_kernel(idx_hbm, w_hbm, tab_hbm, f_ref, tab_vmem, idx_smem, w_smem, sems):
    i = pl.program_id(0)
    slot = _stage_blocks(i, pl.num_programs(0), tab_hbm, tab_vmem,
                         [(idx_hbm, idx_smem), (w_hbm, w_smem)], sems)
    tb = f_ref.shape[0]
    n_acc = 4

    def token(t, carry):
        accs = [jnp.zeros((V7X_SUBLANES, V7X_LANES), F32) for _ in range(n_acc)]
        for r in range(N_SEL):
            row = _load_row(tab_vmem, idx_smem[slot, r, t])
            accs[r % n_acc] = accs[r % n_acc] + w_smem[slot, t, r] * row
        f_ref[t] = (accs[0] + accs[1]) + (accs[2] + accs[3])
        return carry

    lax.fori_loop(0, tb, token, 0)


def _peer_experts(x1, g, idx, u_tab, v_tab, tb):
    t, d = x1.shape
    nblk = t // tb
    x3 = x1.reshape(t, V7X_SUBLANES, d // V7X_SUBLANES)
    any_spec = pl.BlockSpec(memory_space=pl.ANY)
    w = pl.pallas_call(
        _peer_up_kernel,
        grid=(nblk,),
        in_specs=[pl.BlockSpec((tb,) + x3.shape[1:], lambda i: (i, 0, 0)),
                  pl.BlockSpec((tb, N_SEL), lambda i: (i, 0)), any_spec, any_spec],
        out_specs=pl.BlockSpec((tb, N_SEL), lambda i: (i, 0)),
        out_shape=jax.ShapeDtypeStruct((t, N_SEL), F32),
        scratch_shapes=[pltpu.VMEM(u_tab.shape, u_tab.dtype), pltpu.SMEM((2, N_SEL, tb), I32),
                        pltpu.VMEM((tb, N_SEL), F32), pltpu.SemaphoreType.DMA((3,))],
        compiler_params=_cparams("arbitrary"),
        name="peer_up",
    )(x3, g, idx, u_tab)
    f3 = pl.pallas_call(
        _peer_down_kernel,
        grid=(nblk,),
        in_specs=[any_spec, any_spec, any_spec],
        out_specs=pl.BlockSpec((tb,) + x3.shape[1:], lambda i: (i, 0, 0)),
        out_shape=jax.ShapeDtypeStruct(x3.shape, F32),
        scratch_shapes=[pltpu.VMEM(v_tab.shape, v_tab.dtype), pltpu.SMEM((2, N_SEL, tb), I32),
                        pltpu.SMEM((2, tb, N_SEL), F32), pltpu.SemaphoreType.DMA((5,))],
        compiler_params=_cparams("arbitrary"),
        name="peer_down",
    )(idx, w.reshape(nblk, tb, N_SEL), v_tab)
    return f3.reshape(t, d)


def _final_kernel(base_ref, f_ref, g_ref, b_ref, o_ref):
    o_ref[...] = _layer_norm(base_ref[...] + f_ref[...], g_ref[...], b_ref[...])


def _final(base, f, g2, b2):
    t, d = base.shape
    tm = min(ROW_TILE, t)
    row = pl.BlockSpec((tm, d), lambda i: (i, 0))
    vec = pl.BlockSpec((1, d), lambda i: (0, 0))
    return pl.pallas_call(
        _final_kernel,
        grid=(t // tm,),
        in_specs=[row, row, vec, vec],
        out_specs=row,
        out_shape=jax.ShapeDtypeStruct((t, d), F32),
        compiler_params=_cparams("parallel"),
        name="residual_ln2",
    )(base, f, g2, b2)


def _reorder_in_proj(w, b):
    hk = M_HEADS * M_DK
    widths = (hk, hk, M_HEADS * M_DV, M_HEADS * M_DV, M_HEADS, M_HEADS,
              A_HEADS * 2 * A_DK, A_HEADS * 2 * A_DK, A_HEADS * A_DV)
    cuts = [0]
    for n in widths:
        cuts.append(cuts[-1] + n)
    total = w.shape[1]
    d_model = (total - cuts[-1]) // 2
    cuts += [cuts[-1] + d_model, total]
    piece = lambda a, i: a[..., cuts[i]:cuts[i + 1]]
    order = (0, 1, 2, 3, 6, 7, 8, 9, 10, 4, 5)
    pad = V7X_LANES - 2 * M_HEADS
    wr = jnp.concatenate([piece(w, i) for i in order] + [jnp.zeros((w.shape[0], pad), w.dtype)], axis=1)
    br = jnp.concatenate([piece(b, i) for i in order] + [jnp.zeros((pad,), b.dtype)], axis=0)
    return wr.astype(BF16), br.reshape(1, -1)


def _table_tiles(tab):
    n, d = tab.shape
    return tab.astype(BF16).reshape(n, V7X_SUBLANES, d // V7X_SUBLANES)


def _token_local(x2, ya, yb, sga, sgb, p2, lw, alpha):
    t = x2.shape[0]
    if t % PEER_TILE:
        pad = lambda a: jnp.pad(a, ((0, PEER_TILE - t % PEER_TILE), (0, 0)))
        return _token_local(pad(x2), pad(ya), pad(yb), pad(sga), pad(sgb), pad(p2), lw, alpha)[:t]
    x1, base, q = _merge(x2, ya, yb, sga, sgb, p2, lw["wa"], lw["wb"], lw["wo"], lw["g1"], lw["b1"],
                         lw["wg"], lw["wp"], lw["wq"], alpha)
    tb = min(PEER_TILE, x2.shape[0])
    idx, g = _route(q, lw["subkeys"], tb)
    f = _peer_experts(x1, g, idx, lw["u_tab"], lw["v_tab"], tb)
    return _final(base, f, lw["g2"], lw["b2"])


def kernel(x_prompt, x_sample, cache_k, cache_v, state_C, state_n, state_m, page_table, p_prompt, p_sample,
           w_in, b_in, lambda_q1, lambda_k1, lambda_q2, lambda_k2, mlstm_norm_g, diff_norm_g, w_branch_a,
           w_branch_b, w_out, ln1_g, ln1_b, peer_wq, peer_subkeys, peer_u, peer_v, ple_w_gate, ple_w_proj,
           ln2_g, ln2_b):
    depth = w_in.shape[0]
    batch, seq, d_model = x_prompt.shape
    db, dseq, _ = x_sample.shape
    assert dseq == 1, "the decode path handles one new token per request"
    alpha = (2.0 * depth) ** 0.25
    slopes = 2.0 ** (-8.0 * (jnp.arange(A_HEADS, dtype=F32) + 1.0) / A_HEADS)
    slopes_h = jnp.broadcast_to(slopes[:, None, None], (A_HEADS, 1, V7X_LANES))
    slopes_8 = jnp.broadcast_to(jnp.repeat(slopes, 2)[:, None], (2 * A_HEADS, V7X_LANES))

    hp = x_prompt.reshape(batch * seq, d_model)
    hs = x_sample.reshape(db, d_model)
    outs = [[] for _ in range(10)]
    for l in range(depth):
        lam_init = 0.8 - 0.6 * math.exp(-0.3 * l)
        w_l, b_l = _reorder_in_proj(w_in[l], b_in[l])
        row = lambda a: a.reshape(1, -1)
        lw = dict(wa=w_branch_a[l].astype(BF16), wb=w_branch_b[l].astype(BF16), wo=w_out[l].astype(BF16),
                  g1=row(ln1_g[l]), b1=row(ln1_b[l]), wg=ple_w_gate[l].astype(BF16),
                  wp=ple_w_proj[l].astype(BF16), wq=peer_wq[l].astype(BF16),
                  subkeys=peer_subkeys[l].astype(BF16), u_tab=_table_tiles(peer_u[l]),
                  v_tab=_table_tiles(peer_v[l]), g2=row(ln2_g[l]), b2=row(ln2_b[l]))
        lams = [row(a[l].astype(F32)) for a in (lambda_q1, lambda_k1, lambda_q2, lambda_k2)]
        mng = row(mlstm_norm_g[l])
        dng = row(diff_norm_g[l])

        mq, mk, mv, so, aq, akf, avf, akb, avb, sga, sgb, gates = _project(hp, w_l, b_l)
        ya, c_p, n_p, m_p = _mlstm_prompt(mq, mk, mv, so, gates, mng, batch, seq)
        yb = _attn_prompt(aq, akb, avb, slopes_h, dng, lams, batch, seq, lam_init)
        hp = _token_local(hp, ya, yb, sga, sgb, p_prompt[l].reshape(batch * seq, -1), lw, alpha)
        outs[0].append(akf.reshape(batch, seq, A_HEADS, 2, A_DK))
        outs[1].append(avf.reshape(batch, seq, A_HEADS, A_DV))
        outs[2].append(c_p)
        outs[3].append(n_p)
        outs[4].append(m_p[:, 0, :M_HEADS])

        mq, mk, mv, so, aq, akf, avf, akb, avb, sga, sgb, gates = _project(hs, w_l, b_l)
        m0 = jnp.pad(state_m[l].astype(F32), ((0, 0), (0, V7X_LANES - M_HEADS))).reshape(db, 1, V7X_LANES)
        ya, c_s, n_s, m_s = _mlstm_step(mq, mk, mv, so, gates, mng, state_C[l].astype(F32),
                                        state_n[l].astype(F32), m0)
        yb = _attn_decode(page_table, aq, akb, avb, cache_k[l], cache_v[l], slopes_8, dng, lams, lam_init)
        hs = _token_local(hs, ya.reshape(db, -1), yb, sga, sgb, p_sample[l].reshape(db, -1), lw, alpha)
        outs[5].append(akf.reshape(db, 1, A_HEADS, 2, A_DK))
        outs[6].append(avf.reshape(db, 1, A_HEADS, A_DV))
        outs[7].append(c_s)
        outs[8].append(n_s)
        outs[9].append(m_s[:, 0, :M_HEADS])

    st = [jnp.stack(o, 0) for o in outs]
    return (hp.reshape(batch, seq, d_model), hs.reshape(db, 1, d_model),
            st[0], st[1], st[2], st[3], st[4], st[5], st[6], st[7], st[8], st[9])
```

```python
import functools
import math

import jax
import jax.numpy as jnp
from jax import lax
from jax.experimental import pallas as pl
from jax.experimental.pallas import tpu as pltpu

F32 = jnp.float32
BF16 = jnp.bfloat16
I32 = jnp.int32

M_HEADS, M_DK, M_DV = 4, 128, 128
A_HEADS, A_DK, A_DV = 4, 64, 128
P_HEADS, P_NKEYS, P_DHALF, P_TOPK = 8, 128, 128, 16
PAGE_SIZE = 128
LN_EPS = 1e-5
N_SEL = P_HEADS * P_TOPK

V7X_LANES = 128
V7X_SUBLANES = 8
V7X_VMEM_LIMIT = 56 * 1024 * 1024

MLSTM_CHUNK = 128
ATTN_BLOCK = 256
PAGES_PER_STEP = 8
ROW_TILE = 256
PEER_TILE = 128
PEER_GROUP = 32
PEER_CHUNK = 32
ROUTE_TILE = 256

NEG_INF = float("-inf")


def _cparams(*sem):
    return pltpu.CompilerParams(dimension_semantics=sem, vmem_limit_bytes=V7X_VMEM_LIMIT)


def _nt_dot(a, b):
    return lax.dot_general(a, b, (((1,), (1,)), ((), ())), preferred_element_type=F32)


def _dot(a, b):
    return jnp.dot(a, b, preferred_element_type=F32)


_C_MQ, _C_MK, _C_MV, _C_MO, _C_AQ, _C_AK, _C_AV, _C_GA = 0, 512, 1024, 1536, 2048, 2560, 3072, 3584
_C_GB, _C_GATE, _C_END = 4608, 5632, 5760


def _proj_kernel(x_ref, w_ref, b_ref, mq, mk, mv, so, aq, akf, avf, akb, avb, sga, sgb, gates):
    xb = x_ref[...].astype(BF16)

    def seg(lo, hi):
        return _dot(xb, w_ref[:, lo:hi]) + b_ref[:, lo:hi]

    mq[...] = seg(_C_MQ, _C_MK).astype(BF16)
    mk[...] = (seg(_C_MK, _C_MV) * (M_DK ** -0.5)).astype(BF16)
    mv[...] = seg(_C_MV, _C_MO).astype(BF16)
    so[...] = jax.nn.sigmoid(seg(_C_MO, _C_AQ)).astype(BF16)
    aq[...] = (seg(_C_AQ, _C_AK) * (A_DK ** -0.5)).astype(BF16)
    k = seg(_C_AK, _C_AV)
    akf[...] = k
    akb[...] = k.astype(BF16)
    v = seg(_C_AV, _C_GA)
    avf[...] = v
    avb[...] = v.astype(BF16)
    sga[...] = jax.nn.sigmoid(seg(_C_GA, _C_GB)).astype(BF16)
    sgb[...] = jax.nn.sigmoid(seg(_C_GB, _C_GATE)).astype(BF16)
    gates[...] = seg(_C_GATE, _C_END)


def _project(x2, w, b):
    t, d = x2.shape
    tm = min(ROW_TILE, t)
    row = lambda i: (i, 0)
    fixed = lambda i: (0, 0)
    widths = [(512, BF16)] * 5 + [(512, F32)] * 2 + [(512, BF16)] * 2 + [(1024, BF16)] * 2 + [(128, F32)]
    return pl.pallas_call(
        _proj_kernel,
        grid=(t // tm,),
        in_specs=[pl.BlockSpec((tm, d), row), pl.BlockSpec(w.shape, fixed), pl.BlockSpec(b.shape, fixed)],
        out_specs=[pl.BlockSpec((tm, n), row) for n, _ in widths],
        out_shape=[jax.ShapeDtypeStruct((t, n), dt) for n, dt in widths],
        compiler_params=_cparams("parallel"),
        name="project_in",
    )(x2, w, b)


def _head_ln_gate(hh, g_row, so):
    mu = jnp.mean(hh, axis=-1, keepdims=True)
    var = jnp.mean(jnp.square(hh - mu), axis=-1, keepdims=True)
    return ((hh - mu) * lax.rsqrt(var + LN_EPS) * g_row * so.astype(F32)).astype(BF16)


def _mlstm_prompt_kernel(q_ref, k_ref, v_ref, so_ref, g_ref, ng_ref, ya_ref, c_ref, n_ref, m_ref):
    @pl.when(pl.program_id(1) == 0)
    def _():
        c_ref[...] = jnp.zeros_like(c_ref)
        n_ref[...] = jnp.zeros_like(n_ref)
        m_ref[...] = jnp.zeros_like(m_ref)

    L = q_ref.shape[0]
    gate = g_ref[...]
    cum = jax.nn.log_sigmoid(gate)
    row = lax.broadcasted_iota(I32, cum.shape, 0)
    sh = 1
    while sh < L:
        cum = cum + jnp.where(row >= sh, pltpu.roll(cum, sh, axis=0), 0.0)
        sh *= 2
    cum_t = cum.T
    gate_t = gate.T
    causal = lax.broadcasted_iota(I32, (L, L), 1) <= lax.broadcasted_iota(I32, (L, L), 0)
    lane = lax.broadcasted_iota(I32, (1, V7X_LANES), 1)
    m_all = m_ref[0]
    m_next = m_all
    for h in range(M_HEADS):
        hs = slice(h * M_DK, (h + 1) * M_DK)
        b_col = cum[:, M_HEADS + h:M_HEADS + h + 1]
        b_row = cum_t[M_HEADS + h:M_HEADS + h + 1, :]
        i_col = gate[:, h:h + 1]
        i_row = gate_t[h:h + 1, :]
        m0 = m_all[:, h:h + 1]
        dmat = jnp.where(causal, b_col - b_row + i_row, NEG_INF)
        inter = b_col + m0
        m = jnp.maximum(inter, jnp.max(dmat, axis=1, keepdims=True))
        w = jnp.exp(dmat - m)
        a = jnp.exp(inter - m)
        qh, kh, vh = q_ref[:, hs], k_ref[:, hs], v_ref[:, hs]
        c0 = c_ref[0, h]
        n0 = n_ref[0, h:h + 1, :]
        qk = _nt_dot(qh, kh) * w
        num = _dot(qk.astype(BF16), vh) + a * _dot(qh, c0.astype(BF16))
        den = jnp.sum(qk, axis=1, keepdims=True) + a * jnp.sum(qh.astype(F32) * n0, axis=1, keepdims=True)
        hh = num / jnp.maximum(jnp.abs(den), jnp.exp(-m))
        ya_ref[:, hs] = _head_ln_gate(hh, ng_ref[:, hs], so_ref[:, hs])
        m_last = m[L - 1:L, :]
        b_last = b_col[L - 1:L, :]
        w_last = jnp.exp(b_last - b_col + i_col - m_last)
        a_last = jnp.exp(b_last + m0 - m_last)
        kw = kh.astype(F32) * w_last
        c_ref[0, h] = a_last * c0 + _dot(kw.T.astype(BF16), vh)
        n_ref[0, h:h + 1, :] = a_last * n0 + jnp.sum(kw, axis=0, keepdims=True)
        m_next = jnp.where(lane == h, m_last, m_next)
    m_ref[0] = m_next


def _mlstm_prompt(mq, mk, mv, so, gates, ng, batch, seq):
    L = min(MLSTM_CHUNK, seq)
    nc = seq // L
    tok = lambda b, c: (b * nc + c, 0)
    wide = pl.BlockSpec((L, M_HEADS * M_DK), tok)
    return pl.pallas_call(
        _mlstm_prompt_kernel,
        grid=(batch, nc),
        in_specs=[wide, wide, wide, wide, pl.BlockSpec((L, V7X_LANES), tok),
                  pl.BlockSpec((1, M_HEADS * M_DV), lambda b, c: (0, 0))],
        out_specs=[wide,
                   pl.BlockSpec((1, M_HEADS, M_DK, M_DV), lambda b, c: (b, 0, 0, 0)),
                   pl.BlockSpec((1, M_HEADS, M_DK), lambda b, c: (b, 0, 0)),
                   pl.BlockSpec((1, 1, V7X_LANES), lambda b, c: (b, 0, 0))],
        out_shape=[jax.ShapeDtypeStruct((batch * seq, M_HEADS * M_DV), BF16),
                   jax.ShapeDtypeStruct((batch, M_HEADS, M_DK, M_DV), F32),
                   jax.ShapeDtypeStruct((batch, M_HEADS, M_DK), F32),
                   jax.ShapeDtypeStruct((batch, 1, V7X_LANES), F32)],
        compiler_params=_cparams("parallel", "arbitrary"),
        name="mlstm_prompt",
    )(mq, mk, mv, so, gates, ng)


def _mlstm_step_kernel(q_ref, k_ref, v_ref, so_ref, g_ref, ng_ref, c0_ref, n0_ref, m0_ref,
                       ya_ref, c_ref, n_ref, m_ref):
    gate = g_ref[0]
    m_all = m0_ref[0]
    lane = lax.broadcasted_iota(I32, (1, V7X_LANES), 1)
    diag = lax.broadcasted_iota(I32, (M_DK, M_DK), 0) == lax.broadcasted_iota(I32, (M_DK, M_DK), 1)
    m_next = m_all
    for h in range(M_HEADS):
        hs = slice(h * M_DK, (h + 1) * M_DK)
        q = q_ref[0][:, hs].astype(F32)
        k = k_ref[0][:, hs].astype(F32)
        v = v_ref[0][:, hs].astype(F32)
        ig = gate[:, h:h + 1]
        lf = jax.nn.log_sigmoid(gate[:, M_HEADS + h:M_HEADS + h + 1])
        m0 = m_all[:, h:h + 1]
        inter = lf + m0
        m = jnp.maximum(inter, ig)
        w = jnp.exp(ig - m)
        a = jnp.exp(inter - m)
        c0 = c0_ref[0, h]
        n0 = n0_ref[0, h:h + 1, :]
        qk = jnp.sum(q * k, axis=1, keepdims=True) * w
        q8 = jnp.broadcast_to(q, (V7X_SUBLANES, M_DK)).astype(BF16)
        qc = _dot(q8, c0.astype(BF16))[0:1, :]
        num = qk * v + a * qc
        den = qk + a * jnp.sum(q * n0, axis=1, keepdims=True)
        hh = num / jnp.maximum(jnp.abs(den), jnp.exp(-m))
        ya_ref[0, :, hs] = _head_ln_gate(hh, ng_ref[:, hs], so_ref[0][:, hs])
        kd = jnp.where(diag, jnp.broadcast_to(k, (M_DK, M_DK)), 0.0).astype(BF16)
        vb = jnp.broadcast_to(v, (M_DK, M_DV)).astype(BF16)
        c_ref[0, h] = a * c0 + w * _dot(kd, vb)
        n_ref[0, h:h + 1, :] = a * n0 + w * k
        m_next = jnp.where(lane == h, m, m_next)
    m_ref[0] = m_next


def _mlstm_step(mq, mk, mv, so, gates, ng, c0, n0, m0):
    db = mq.shape[0]
    r3 = lambda a: a.reshape(db, 1, a.shape[-1])
    tok = lambda b: (b, 0, 0)
    wide = pl.BlockSpec((1, 1, M_HEADS * M_DK), tok)
    narrow = pl.BlockSpec((1, 1, V7X_LANES), tok)
    cspec = pl.BlockSpec((1, M_HEADS, M_DK, M_DV), lambda b: (b, 0, 0, 0))
    nspec = pl.BlockSpec((1, M_HEADS, M_DK), tok)
    return pl.pallas_call(
        _mlstm_step_kernel,
        grid=(db,),
        in_specs=[wide, wide, wide, wide, narrow, pl.BlockSpec((1, M_HEADS * M_DV), lambda b: (0, 0)),
                  cspec, nspec, narrow],
        out_specs=[wide, cspec, nspec, narrow],
        out_shape=[jax.ShapeDtypeStruct((db, 1, M_HEADS * M_DV), BF16),
                   jax.ShapeDtypeStruct(c0.shape, F32),
                   jax.ShapeDtypeStruct(n0.shape, F32),
                   jax.ShapeDtypeStruct((db, 1, V7X_LANES), F32)],
        compiler_params=_cparams("parallel"),
        name="mlstm_step",
    )(r3(mq), r3(mk), r3(mv), r3(so), r3(gates), ng, c0, n0, m0)


def _lambda_value(lq1, lk1, lq2, lk2, lam_init):
    s1 = jnp.sum(lq1[...] * lk1[...], axis=1, keepdims=True)
    s2 = jnp.sum(lq2[...] * lk2[...], axis=1, keepdims=True)
    return jnp.exp(s1) - jnp.exp(s2) + lam_init


def _rms_gain(o, g_row, lam_init):
    return o * lax.rsqrt(jnp.mean(jnp.square(o), axis=-1, keepdims=True) + LN_EPS) * g_row * (1.0 - lam_init)


def _attn_prompt_kernel(q_ref, k_ref, v_ref, slope_ref, ng_ref, lq1, lk1, lq2, lk2, o_ref, *, lam_init):
    tq = q_ref.shape[0]
    qi = pl.program_id(2)
    q = q_ref[...]
    lane = lax.broadcasted_iota(I32, q.shape, 1)
    zero = jnp.zeros_like(q)
    qq = jnp.concatenate([jnp.where(lane < A_DK, q, zero), jnp.where(lane >= A_DK, q, zero)], axis=0)
    slope = slope_ref[0]
    slope_row = jnp.concatenate([slope] * (tq // V7X_LANES), axis=1)
    col = lax.broadcasted_iota(I32, (1, tq), 1)

    def block(j, carry, masked):
        m, l, acc = carry
        start = pl.multiple_of(j * tq, tq)
        kb = k_ref[pl.ds(start, tq), :]
        vb = v_ref[pl.ds(start, tq), :]
        s = _nt_dot(qq, kb) + slope_row * (j * tq + col).astype(F32)
        if masked:
            r = lax.broadcasted_iota(I32, s.shape, 0)
            r = jnp.where(r >= tq, r - tq, r)
            s = jnp.where(lax.broadcasted_iota(I32, s.shape, 1) <= r, s, NEG_INF)
        m_new = jnp.maximum(m, jnp.max(s, axis=1, keepdims=True))
        p = jnp.exp(s - m_new)
        alpha = jnp.exp(m - m_new)
        l = alpha * l + jnp.sum(p, axis=1, keepdims=True)
        acc = alpha * acc + _dot(p.astype(BF16), vb)
        return m_new, l, acc

    init = (jnp.full((2 * tq, 1), NEG_INF, F32), jnp.zeros((2 * tq, 1), F32), jnp.zeros((2 * tq, A_DV), F32))
    carry = lax.fori_loop(0, qi, lambda j, c: block(j, c, False), init)
    _, l, acc = block(qi, carry, True)
    lam = _lambda_value(lq1, lk1, lq2, lk2, lam_init)
    o = acc[:tq] / l[:tq] - lam * (acc[tq:] / l[tq:])
    o_ref[...] = _rms_gain(o, ng_ref[...], lam_init).astype(BF16)


def _attn_prompt(aq, akb, avb, slopes, ng, lams, batch, seq, lam_init):
    tq = min(ATTN_BLOCK, seq)
    nq = seq // tq
    hw = 2 * A_DK
    lam_spec = pl.BlockSpec((1, A_DK), lambda b, h, i: (0, 0))
    return pl.pallas_call(
        functools.partial(_attn_prompt_kernel, lam_init=lam_init),
        grid=(batch, A_HEADS, nq),
        in_specs=[pl.BlockSpec((tq, hw), lambda b, h, i: (b * nq + i, h)),
                  pl.BlockSpec((seq, hw), lambda b, h, i: (b, h)),
                  pl.BlockSpec((seq, A_DV), lambda b, h, i: (b, h)),
                  pl.BlockSpec((1, 1, V7X_LANES), lambda b, h, i: (h, 0, 0)),
                  pl.BlockSpec((1, A_DV), lambda b, h, i: (0, h)),
                  lam_spec, lam_spec, lam_spec, lam_spec],
        out_specs=pl.BlockSpec((tq, A_DV), lambda b, h, i: (b * nq + i, h)),
        out_shape=jax.ShapeDtypeStruct((batch * seq, A_HEADS * A_DV), BF16),
        compiler_params=_cparams("parallel", "parallel", "arbitrary"),
        name="diff_attn_prompt",
    )(aq, akb, avb, slopes, ng, *lams)


def _attn_decode_kernel(pt_ref, q_ref, ks_ref, vs_ref, slope_ref, ng_ref, lq1, lk1, lq2, lk2, *rest,
                        lam_init, pps):
    k_refs, v_refs = rest[:pps], rest[pps:2 * pps]
    o_ref, m_ref, l_ref, acc_ref = rest[2 * pps:]
    j = pl.program_id(1)
    nmap = 2 * A_HEADS
    width = A_HEADS * 2 * A_DK

    @pl.when(j == 0)
    def _():
        m_ref[...] = jnp.full(m_ref.shape, NEG_INF, F32)
        l_ref[...] = jnp.zeros_like(l_ref)
        acc_ref[...] = jnp.zeros_like(acc_ref)

    r8 = lax.broadcasted_iota(I32, (nmap, width), 0)
    c8 = lax.broadcasted_iota(I32, (nmap, width), 1)
    own = (c8 // A_DK) == r8
    qmat = jnp.where(own, jnp.broadcast_to(q_ref[0].astype(F32), (nmap, width)), 0.0)
    qb = qmat.astype(BF16)
    slope = slope_ref[...]
    lane = lax.broadcasted_iota(I32, (nmap, pps * PAGE_SIZE), 1)

    def fold(s):
        m_old = m_ref[...]
        m_new = jnp.maximum(m_old, jnp.max(s, axis=1, keepdims=True))
        alpha = jnp.exp(m_old - m_new)
        p = jnp.exp(s - m_new[:, 0:1])
        m_ref[...] = m_new
        return alpha, p

    q_maps = [jnp.broadcast_to(qb[r:r + 1, r * A_DK:(r + 1) * A_DK], (V7X_SUBLANES, A_DK)) for r in range(nmap)]
    s = jnp.concatenate(
        [jnp.concatenate([_dot(q_maps[r], k_refs[i][0, r // 2, r % 2].astype(BF16))[0:1] for r in range(nmap)],
                         axis=0) for i in range(pps)], axis=1)
    s = s + slope[:, 0:1] * (j * (pps * PAGE_SIZE) + lane).astype(F32)
    alpha, p = fold(s)
    l_ref[...] = alpha * l_ref[...] + jnp.sum(p, axis=1, keepdims=True)
    pv = []
    for h in range(A_HEADS):
        parts = [_dot(p[:, i * PAGE_SIZE:(i + 1) * PAGE_SIZE], v_refs[i][0, pl.ds(h, PAGE_SIZE, stride=A_HEADS), :])
                 for i in range(pps)]
        while len(parts) > 1:
            parts = [a + b for a, b in zip(parts[0::2], parts[1::2])]
        pv.append(parts[0])
    acc_ref[...] = alpha[:, 0:1] * acc_ref[...] + jnp.concatenate(pv, axis=1)

    @pl.when(j == pl.num_programs(1) - 1)
    def _():
        past = jnp.asarray(pl.num_programs(1) * pps * PAGE_SIZE, F32)
        k_new = ks_ref[0].astype(F32)
        v_new = vs_ref[0].astype(F32)
        s = jnp.sum(qmat * k_new, axis=1, keepdims=True) + slope[:, 0:1] * past
        alpha, p = fold(s)
        l = alpha * l_ref[...] + p
        acc = alpha[:, 0:1] * acc_ref[...] + p * v_new
        o = acc / l[:, 0:1]
        lam = _lambda_value(lq1, lk1, lq2, lk2, lam_init)
        d = o - lam * pltpu.roll(o, nmap - 1, axis=0)
        keep = r8 == 2 * (c8 // A_DV)
        hb = jnp.sum(jnp.where(keep, d, 0.0), axis=0, keepdims=True)
        for h in range(A_HEADS):
            hs = slice(h * A_DV, (h + 1) * A_DV)
            o_ref[0, :, hs] = _rms_gain(hb[:, hs], ng_ref[:, hs], lam_init).astype(BF16)


def _attn_decode(page_table, aq, akb, avb, cache_k, cache_v, slopes8, ng, lams, lam_init):
    db, n_pages = page_table.shape
    pps = math.gcd(PAGES_PER_STEP, n_pages)
    width = A_HEADS * 2 * A_DK
    r3 = lambda a: a.reshape(db, 1, a.shape[-1])
    tok = lambda b, j, pt: (b, 0, 0)
    fixed = lambda b, j, pt: (0, 0)
    cache_k = jnp.transpose(cache_k, (0, 2, 3, 4, 1))
    k_specs = [pl.BlockSpec((1,) + cache_k.shape[1:], lambda b, j, pt, i=i: (pt[b, j * pps + i], 0, 0, 0, 0))
               for i in range(pps)]
    cache_v = cache_v.reshape(cache_v.shape[0], PAGE_SIZE * A_HEADS, A_DV)
    v_specs = [pl.BlockSpec((1,) + cache_v.shape[1:], lambda b, j, pt, i=i: (pt[b, j * pps + i], 0, 0))
               for i in range(pps)]
    lam_spec = pl.BlockSpec((1, A_DK), fixed)
    grid_spec = pltpu.PrefetchScalarGridSpec(
        num_scalar_prefetch=1,
        grid=(db, n_pages // pps),
        in_specs=[pl.BlockSpec((1, 1, width), tok), pl.BlockSpec((1, 1, width), tok),
                  pl.BlockSpec((1, 1, width), tok),
                  pl.BlockSpec((2 * A_HEADS, V7X_LANES), fixed), pl.BlockSpec((1, width), fixed),
                  lam_spec, lam_spec, lam_spec, lam_spec] + k_specs + v_specs,
        out_specs=pl.BlockSpec((1, 1, width), tok),
        scratch_shapes=[pltpu.VMEM((2 * A_HEADS, V7X_LANES), F32), pltpu.VMEM((2 * A_HEADS, V7X_LANES), F32),
                        pltpu.VMEM((2 * A_HEADS, width), F32)],
    )
    out = pl.pallas_call(
        functools.partial(_attn_decode_kernel, lam_init=lam_init, pps=pps),
        grid_spec=grid_spec,
        out_shape=jax.ShapeDtypeStruct((db, 1, width), BF16),
        compiler_params=_cparams("parallel", "arbitrary"),
        name="diff_attn_decode",
    )(page_table, r3(aq), r3(akb), r3(avb), slopes8, ng, *lams, *([cache_k] * pps), *([cache_v] * pps))
    return out.reshape(db, width)


def _layer_norm(x, g, b):
    mu = jnp.mean(x, axis=-1, keepdims=True)
    var = jnp.mean(jnp.square(x - mu), axis=-1, keepdims=True)
    return (x - mu) * lax.rsqrt(var + LN_EPS) * g + b


def _merge_kernel(x_ref, ya_ref, yb_ref, sga_ref, sgb_ref, p_ref, wa_ref, wb_ref, wo_ref, g1_ref, b1_ref,
                  wg_ref, wp_ref, wq_ref, x1_ref, base_ref, q_ref, *, alpha):
    ya = _dot(ya_ref[...], wa_ref[...])
    yb = _dot(yb_ref[...], wb_ref[...])
    y = sga_ref[...].astype(F32) * ya + sgb_ref[...].astype(F32) * yb
    y = _dot(y.astype(BF16), wo_ref[...])
    x1 = _layer_norm(alpha * x_ref[...] + y, g1_ref[...], b1_ref[...])
    x1_ref[...] = x1
    x1b = x1.astype(BF16)
    pe = jax.nn.sigmoid(_dot(x1b, wg_ref[...])) * _dot(p_ref[...].astype(BF16), wp_ref[...])
    base_ref[...] = alpha * x1 + pe
    q_ref[...] = _dot(x1b, wq_ref[...]).astype(BF16)


def _merge(x2, ya, yb, sga, sgb, p2, wa, wb, wo, g1, b1, wg, wp, wq, alpha):
    t, d = x2.shape
    tm = min(ROW_TILE, t)
    row = lambda i: (i, 0)
    fixed = lambda i: (0, 0)
    rows = lambda a: pl.BlockSpec((tm, a.shape[1]), row)
    full = lambda a: pl.BlockSpec(a.shape, fixed)
    nq = wq.shape[1]
    return pl.pallas_call(
        functools.partial(_merge_kernel, alpha=alpha),
        grid=(t // tm,),
        in_specs=[rows(x2), rows(ya), rows(yb), rows(sga), rows(sgb), rows(p2),
                  full(wa), full(wb), full(wo), full(g1), full(b1), full(wg), full(wp), full(wq)],
        out_specs=[pl.BlockSpec((tm, d), row), pl.BlockSpec((tm, d), row), pl.BlockSpec((tm, nq), row)],
        out_shape=[jax.ShapeDtypeStruct((t, d), F32), jax.ShapeDtypeStruct((t, d), F32),
                   jax.ShapeDtypeStruct((t, nq), BF16)],
        compiler_params=_cparams("parallel"),
        name="merge_ln1",
    )(x2, ya, yb, sga, sgb, p2, wa, wb, wo, g1, b1, wg, wp, wq)


def _topk_rows(s, k, payload=None):
    n = s.shape[0]
    rows = lax.broadcasted_iota(I32, s.shape, 0).astype(F32)
    vals, picks = [], []
    for _ in range(k):
        mx = jnp.max(s, axis=0, keepdims=True)
        idx = jnp.min(jnp.where(s == mx, rows, float(n)), axis=0, keepdims=True)
        sel = rows == idx
        if payload is None:
            picks.append(idx)
        else:
            picks.append(jnp.sum(jnp.where(sel, payload, 0.0), axis=0, keepdims=True))
        vals.append(mx)
        s = jnp.where(sel, NEG_INF, s)
    return jnp.concatenate(vals, axis=0), jnp.concatenate(picks, axis=0)


def _pair_candidates(sv, si):
    sub = lax.broadcasted_iota(I32, (V7X_SUBLANES,) + sv[0].shape[1:], 0)
    pair = lambda x, y, mul: x * mul + y
    vals = [sv[0][0:1] + sv[1], sv[0][1:2] + sv[1][0:V7X_SUBLANES]]
    idxs = [pair(si[0][0:1], si[1], float(P_NKEYS)), pair(si[0][1:2], si[1][0:V7X_SUBLANES], float(P_NKEYS))]
    for a in range(2, V7X_SUBLANES):
        keep = sub < P_TOPK // (a + 1)
        vals.append(jnp.where(keep, sv[0][a:a + 1] + sv[1][0:V7X_SUBLANES], NEG_INF))
        idxs.append(pair(si[0][a:a + 1], si[1][0:V7X_SUBLANES], float(P_NKEYS)))
    vals.append(sv[0][V7X_SUBLANES:] + sv[1][0:1])
    idxs.append(pair(si[0][V7X_SUBLANES:], si[1][0:1], float(P_NKEYS)))
    return jnp.concatenate(vals, axis=0), jnp.concatenate(idxs, axis=0)


def _route_kernel(q_ref, sk_ref, idx_ref, g_ref, et_ref, gt_ref):
    h = pl.program_id(1)
    rows = pl.ds(pl.multiple_of(h * P_TOPK, P_TOPK), P_TOPK)
    for part in range(q_ref.shape[0] // V7X_LANES):
        ts = slice(part * V7X_LANES, (part + 1) * V7X_LANES)
        sv, si = [], []
        for c in range(2):
            s = _nt_dot(sk_ref[0, c], q_ref[ts, c * P_DHALF:(c + 1) * P_DHALF])
            v, i = _topk_rows(s, P_TOPK)
            sv.append(v)
            si.append(i)
        cand, cidx = _pair_candidates(sv, si)
        cv, eidx = _topk_rows(cand, P_TOPK, payload=cidx)
        e = jnp.exp(cv - cv[0:1])
        et_ref[rows, ts] = eidx
        gt_ref[rows, ts] = e / jnp.sum(e, axis=0, keepdims=True)

    @pl.when(h == P_HEADS - 1)
    def _():
        idx_ref[...] = et_ref[...].T.astype(I32)
        g_ref[...] = gt_ref[...].T


def _route(q, subkeys, tm):
    t = q.shape[0]
    out = pl.BlockSpec((tm, N_SEL), lambda i, h: (i, 0))
    return pl.pallas_call(
        _route_kernel,
        grid=(t // tm, P_HEADS),
        in_specs=[pl.BlockSpec((tm, 2 * P_DHALF), lambda i, h: (i, h)),
                  pl.BlockSpec((1, 2, P_NKEYS, P_DHALF), lambda i, h: (h, 0, 0, 0))],
        out_specs=[out, out],
        out_shape=[jax.ShapeDtypeStruct((t, N_SEL), I32), jax.ShapeDtypeStruct((t, N_SEL), F32)],
        scratch_shapes=[pltpu.VMEM((N_SEL, tm), F32), pltpu.VMEM((N_SEL, tm), F32)],
        compiler_params=_cparams("parallel", "arbitrary"),
        name="peer_route",
    )(q, subkeys)


def _load_row(tab_ref, e):
    return tab_ref[e].astype(F32)


def _sublane_merge(x, y, sh, mask):
    c = jnp.where(mask, x, y)
    d = jnp.where(mask, y, x)
    if 2 * sh == V7X_SUBLANES:
        return c + pltpu.roll(d, sh, axis=0)
    return c + jnp.where(mask, pltpu.roll(d, V7X_SUBLANES - sh, axis=0), pltpu.roll(d, sh, axis=0))


def _stage_blocks(i, nblk, tab_hbm, tab_vmem, per_block, sems):
    slot = i % 2

    def copies(blk, sl):
        return [pltpu.make_async_copy(h.at[blk], s.at[pl.ds(sl * h.shape[1], h.shape[1])],
                                      sems.at[1 + 2 * n + sl])
                for n, (h, s) in enumerate(per_block)]

    @pl.when(i == 0)
    def _():
        table = pltpu.make_async_copy(tab_hbm, tab_vmem, sems.at[0])
        table.start()
        for c in copies(0, 0):
            c.start()
        table.wait()

    for c in copies(i, slot):
        c.wait()

    @pl.when(i + 1 < nblk)
    def _():
        for c in copies(i + 1, 1 - slot):
            c.start()

    return slot


def _peer_up_kernel(x_ref, g_ref, idx_hbm, tab_hbm, w_ref, tab_vmem, idx_smem, part_ref, sems):
    i = pl.program_id(0)
    slot = _stage_blocks(i, pl.num_programs(0), tab_hbm, tab_vmem, [(idx_hbm, idx_smem)], sems)
    tb = x_ref.shape[0]
    chunk = part_ref.shape[0] // N_SEL
    sub = lax.broadcasted_iota(I32, (V7X_SUBLANES, V7X_LANES), 0)
    masks = {sh: (sub & sh) == 0 for sh in (4, 2, 1)}
    ones = jnp.ones((V7X_LANES, V7X_LANES), BF16)
    diag = (lax.broadcasted_iota(I32, (1, N_SEL, V7X_LANES), 1)
            == lax.broadcasted_iota(I32, (1, N_SEL, V7X_LANES), 2))

    def token_chunk(c, carry):
        def token(tt, carry):
            t = c * chunk + tt
            xt = x_ref[t]
            tok_base = (slot * tb + t) * N_SEL

            def group(j, carry):
                col = pl.multiple_of(j * PEER_GROUP, PEER_GROUP)
                for k in range(PEER_GROUP // V7X_SUBLANES):
                    first = col + k * V7X_SUBLANES
                    parts = [_load_row(tab_vmem, idx_smem[tok_base + first + r]) * xt
                             for r in range(V7X_SUBLANES)]
                    for sh in (4, 2, 1):
                        parts = [_sublane_merge(parts[a], parts[a + sh], sh, masks[sh])
                                 for a in range(len(parts)) if (a & sh) == 0]
                    part_ref[pl.ds(pl.multiple_of(tt * N_SEL + first, V7X_SUBLANES), V7X_SUBLANES), :] = parts[0]
                return carry

            return lax.fori_loop(0, N_SEL // PEER_GROUP, group, carry)

        lax.fori_loop(0, chunk, token, 0)
        part = part_ref[...]
        hi = part.astype(BF16)
        lo = (part - hi.astype(F32)).astype(BF16)
        tot = (_dot(hi, ones) + _dot(lo, ones)).reshape(chunk, N_SEL, V7X_LANES)
        s = jnp.sum(jnp.where(diag, tot, 0.0), axis=1)
        act = 0.5 * s * (1.0 + lax.erf(s * (2.0 ** -0.5)))
        rows = pl.ds(pl.multiple_of(c * chunk, chunk), chunk)
        w_ref[rows, :] = g_ref[rows, :] * act
        return carry

    lax.fori_loop(0, tb // chunk, token_chunk, 0)


def _peer_down_kernel(idx_hbm, w_hbm, tab_hbm, f_ref, tab_vmem, idx_smem, w_smem, sems):
    i = pl.program_id(0)
    slot = _stage_blocks(i, pl.num_programs(0), tab_hbm, tab_vmem,
                         [(idx_hbm, idx_smem), (w_hbm, w_smem)], sems)
    tb = f_ref.shape[0]
    n_acc = 4
    zero = jnp.zeros((V7X_SUBLANES, V7X_LANES), F32)

    def token(t, carry):
        tok_base = (slot * tb + t) * N_SEL

        def group(j, accs):
            base = tok_base + j * PEER_GROUP
            accs = list(accs)
            for r in range(PEER_GROUP):
                row = _load_row(tab_vmem, idx_smem[base + r])
                accs[r % n_acc] = accs[r % n_acc] + w_smem[base + r] * row
            return tuple(accs)

        accs = lax.fori_loop(0, N_SEL // PEER_GROUP, group, (zero,) * n_acc)
        f_ref[t] = (accs[0] + accs[1]) + (accs[2] + accs[3])
        return carry

    lax.fori_loop(0, tb, token, 0)


def _peer_experts(x1, g, idx, u_tab, v_tab, tb):
    t, d = x1.shape
    nblk = t // tb
    x3 = x1.reshape(t, V7X_SUBLANES, d // V7X_SUBLANES)
    idx = idx.reshape(nblk, tb * N_SEL)
    any_spec = pl.BlockSpec(memory_space=pl.ANY)
    w = pl.pallas_call(
        _peer_up_kernel,
        grid=(nblk,),
        in_specs=[pl.BlockSpec((tb,) + x3.shape[1:], lambda i: (i, 0, 0)),
                  pl.BlockSpec((tb, N_SEL), lambda i: (i, 0)), any_spec, any_spec],
        out_specs=pl.BlockSpec((tb, N_SEL), lambda i: (i, 0)),
        out_shape=jax.ShapeDtypeStruct((t, N_SEL), F32),
        scratch_shapes=[pltpu.VMEM(u_tab.shape, u_tab.dtype), pltpu.SMEM((2 * tb * N_SEL,), I32),
                        pltpu.VMEM((PEER_CHUNK * N_SEL, V7X_LANES), F32), pltpu.SemaphoreType.DMA((3,))],
        compiler_params=_cparams("arbitrary"),
        name="peer_up",
    )(x3, g, idx, u_tab)
    f3 = pl.pallas_call(
        _peer_down_kernel,
        grid=(nblk,),
        in_specs=[any_spec, any_spec, any_spec],
        out_specs=pl.BlockSpec((tb,) + x3.shape[1:], lambda i: (i, 0, 0)),
        out_shape=jax.ShapeDtypeStruct(x3.shape, F32),
        scratch_shapes=[pltpu.VMEM(v_tab.shape, v_tab.dtype), pltpu.SMEM((2 * tb * N_SEL,), I32),
                        pltpu.SMEM((2 * tb * N_SEL,), F32), pltpu.SemaphoreType.DMA((5,))],
        compiler_params=_cparams("arbitrary"),
        name="peer_down",
    )(idx, w.reshape(nblk, tb * N_SEL), v_tab)
    return f3.reshape(t, d)


def _final_kernel(base_ref, f_ref, g_ref, b_ref, o_ref):
    o_ref[...] = _layer_norm(base_ref[...] + f_ref[...], g_ref[...], b_ref[...])


def _final(base, f, g2, b2):
    t, d = base.shape
    tm = min(ROW_TILE, t)
    row = pl.BlockSpec((tm, d), lambda i: (i, 0))
    vec = pl.BlockSpec((1, d), lambda i: (0, 0))
    return pl.pallas_call(
        _final_kernel,
        grid=(t // tm,),
        in_specs=[row, row, vec, vec],
        out_specs=row,
        out_shape=jax.ShapeDtypeStruct((t, d), F32),
        compiler_params=_cparams("parallel"),
        name="residual_ln2",
    )(base, f, g2, b2)


def _reorder_in_proj(w, b):
    hk = M_HEADS * M_DK
    widths = (hk, hk, M_HEADS * M_DV, M_HEADS * M_DV, M_HEADS, M_HEADS,
              A_HEADS * 2 * A_DK, A_HEADS * 2 * A_DK, A_HEADS * A_DV)
    cuts = [0]
    for n in widths:
        cuts.append(cuts[-1] + n)
    total = w.shape[1]
    d_model = (total - cuts[-1]) // 2
    cuts += [cuts[-1] + d_model, total]
    piece = lambda a, i: a[..., cuts[i]:cuts[i + 1]]
    order = (0, 1, 2, 3, 6, 7, 8, 9, 10, 4, 5)
    pad = V7X_LANES - 2 * M_HEADS
    wr = jnp.concatenate([piece(w, i) for i in order] + [jnp.zeros((w.shape[0], pad), w.dtype)], axis=1)
    br = jnp.concatenate([piece(b, i) for i in order] + [jnp.zeros((pad,), b.dtype)], axis=0)
    return wr.astype(BF16), br.reshape(1, -1)


def _table_tiles(tab):
    n, d = tab.shape
    return tab.astype(BF16).reshape(n, V7X_SUBLANES, d // V7X_SUBLANES)


def _token_local(x2, ya, yb, sga, sgb, p2, lw, alpha):
    t = x2.shape[0]
    if t % PEER_TILE:
        pad = lambda a: jnp.pad(a, ((0, PEER_TILE - t % PEER_TILE), (0, 0)))
        return _token_local(pad(x2), pad(ya), pad(yb), pad(sga), pad(sgb), pad(p2), lw, alpha)[:t]
    x1, base, q = _merge(x2, ya, yb, sga, sgb, p2, lw["wa"], lw["wb"], lw["wo"], lw["g1"], lw["b1"],
                         lw["wg"], lw["wp"], lw["wq"], alpha)
    idx, g = _route(q, lw["subkeys"], ROUTE_TILE if t % ROUTE_TILE == 0 else PEER_TILE)
    f = _peer_experts(x1, g, idx, lw["u_tab"], lw["v_tab"], PEER_TILE)
    return _final(base, f, lw["g2"], lw["b2"])


def kernel(x_prompt, x_sample, cache_k, cache_v, state_C, state_n, state_m, page_table, p_prompt, p_sample,
           w_in, b_in, lambda_q1, lambda_k1, lambda_q2, lambda_k2, mlstm_norm_g, diff_norm_g, w_branch_a,
           w_branch_b, w_out, ln1_g, ln1_b, peer_wq, peer_subkeys, peer_u, peer_v, ple_w_gate, ple_w_proj,
           ln2_g, ln2_b):
    depth = w_in.shape[0]
    batch, seq, d_model = x_prompt.shape
    db, dseq, _ = x_sample.shape
    assert dseq == 1, "the decode path handles one new token per request"
    alpha = (2.0 * depth) ** 0.25
    slopes = 2.0 ** (-8.0 * (jnp.arange(A_HEADS, dtype=F32) + 1.0) / A_HEADS)
    slopes_h = jnp.broadcast_to(slopes[:, None, None], (A_HEADS, 1, V7X_LANES))
    slopes_8 = jnp.broadcast_to(jnp.repeat(slopes, 2)[:, None], (2 * A_HEADS, V7X_LANES))

    hp = x_prompt.reshape(batch * seq, d_model)
    hs = x_sample.reshape(db, d_model)
    outs = [[] for _ in range(10)]
    for l in range(depth):
        lam_init = 0.8 - 0.6 * math.exp(-0.3 * l)
        w_l, b_l = _reorder_in_proj(w_in[l], b_in[l])
        row = lambda a: a.reshape(1, -1)
        lw = dict(wa=w_branch_a[l].astype(BF16), wb=w_branch_b[l].astype(BF16), wo=w_out[l].astype(BF16),
                  g1=row(ln1_g[l]), b1=row(ln1_b[l]), wg=ple_w_gate[l].astype(BF16),
                  wp=ple_w_proj[l].astype(BF16), wq=peer_wq[l].astype(BF16),
                  subkeys=peer_subkeys[l].astype(BF16), u_tab=_table_tiles(peer_u[l]),
                  v_tab=_table_tiles(peer_v[l]), g2=row(ln2_g[l]), b2=row(ln2_b[l]))
        lams = [row(a[l].astype(F32)) for a in (lambda_q1, lambda_k1, lambda_q2, lambda_k2)]
        mng = row(mlstm_norm_g[l])
        dng = row(diff_norm_g[l])

        mq, mk, mv, so, aq, akf, avf, akb, avb, sga, sgb, gates = _project(hp, w_l, b_l)
        ya, c_p, n_p, m_p = _mlstm_prompt(mq, mk, mv, so, gates, mng, batch, seq)
        yb = _attn_prompt(aq, akb, avb, slopes_h, dng, lams, batch, seq, lam_init)
        hp = _token_local(hp, ya, yb, sga, sgb, p_prompt[l].reshape(batch * seq, -1), lw, alpha)
        outs[0].append(akf.reshape(batch, seq, A_HEADS, 2, A_DK))
        outs[1].append(avf.reshape(batch, seq, A_HEADS, A_DV))
        outs[2].append(c_p)
        outs[3].append(n_p)
        outs[4].append(m_p[:, 0, :M_HEADS])

        mq, mk, mv, so, aq, akf, avf, akb, avb, sga, sgb, gates = _project(hs, w_l, b_l)
        m0 = jnp.pad(state_m[l].astype(F32), ((0, 0), (0, V7X_LANES - M_HEADS))).reshape(db, 1, V7X_LANES)
        ya, c_s, n_s, m_s = _mlstm_step(mq, mk, mv, so, gates, mng, state_C[l].astype(F32),
                                        state_n[l].astype(F32), m0)
        yb = _attn_decode(page_table, aq, akb, avb, cache_k[l], cache_v[l], slopes_8, dng, lams, lam_init)
        hs = _token_local(hs, ya.reshape(db, -1), yb, sga, sgb, p_sample[l].reshape(db, -1), lw, alpha)
        outs[5].append(akf.reshape(db, 1, A_HEADS, 2, A_DK))
        outs[6].append(avf.reshape(db, 1, A_HEADS, A_DV))
        outs[7].append(c_s)
        outs[8].append(n_s)
        outs[9].append(m_s[:, 0, :M_HEADS])

    st = [jnp.stack(o, 0) for o in outs]
    return (hp.reshape(batch, seq, d_model), hs.reshape(db, 1, d_model),
            st[0], st[1], st[2], st[3], st[4], st[5], st[6], st[7], st[8], st[9])
```

```python
import functools
import math

import jax
import jax.numpy as jnp
from jax import lax
from jax.experimental import pallas as pl
from jax.experimental.pallas import tpu as pltpu

F32 = jnp.float32
BF16 = jnp.bfloat16
I32 = jnp.int32

M_HEADS, M_DK, M_DV = 4, 128, 128
A_HEADS, A_DK, A_DV = 4, 64, 128
P_HEADS, P_NKEYS, P_DHALF, P_TOPK = 8, 128, 128, 16
PAGE_SIZE = 128
LN_EPS = 1e-5
N_SEL = P_HEADS * P_TOPK

V7X_LANES = 128
V7X_SUBLANES = 8
V7X_VMEM_LIMIT = 56 * 1024 * 1024

MLSTM_CHUNK = 128
ATTN_BLOCK = 256
PAGES_PER_STEP = 8
ROW_TILE = 256
PEER_TILE = 128
PEER_GROUP = 128
PEER_GROUP_DOWN = 16
PEER_CHUNK = 32
ROUTE_TILE = 256

NEG_INF = float("-inf")


def _cparams(*sem):
    return pltpu.CompilerParams(dimension_semantics=sem, vmem_limit_bytes=V7X_VMEM_LIMIT)


def _nt_dot(a, b):
    return lax.dot_general(a, b, (((1,), (1,)), ((), ())), preferred_element_type=F32)


def _dot(a, b):
    return jnp.dot(a, b, preferred_element_type=F32)


_C_MQ, _C_MK, _C_MV, _C_MO, _C_AQ, _C_AK, _C_AV, _C_GA = 0, 512, 1024, 1536, 2048, 2560, 3072, 3584
_C_GB, _C_GATE, _C_END = 4608, 5632, 5760


def _proj_kernel(x_ref, w_ref, b_ref, mq, mk, mv, so, aq, akf, avf, akb, avb, sga, sgb, gates, *, k_by_position):
    xb = x_ref[...].astype(BF16)

    def seg(lo, hi):
        return _dot(xb, w_ref[:, lo:hi]) + b_ref[:, lo:hi]

    mq[...] = seg(_C_MQ, _C_MK).astype(BF16)
    mk[...] = (seg(_C_MK, _C_MV) * (M_DK ** -0.5)).astype(BF16)
    mv[...] = seg(_C_MV, _C_MO).astype(BF16)
    so[...] = jax.nn.sigmoid(seg(_C_MO, _C_AQ)).astype(BF16)
    aq[...] = (seg(_C_AQ, _C_AK) * (A_DK ** -0.5)).astype(BF16)
    k = seg(_C_AK, _C_AV)
    if k_by_position:
        akf[0] = k.T
    else:
        akf[...] = k
    akb[...] = k.astype(BF16)
    v = seg(_C_AV, _C_GA)
    avf[...] = v
    avb[...] = v.astype(BF16)
    sga[...] = jax.nn.sigmoid(seg(_C_GA, _C_GB)).astype(BF16)
    sgb[...] = jax.nn.sigmoid(seg(_C_GB, _C_GATE)).astype(BF16)
    gates[...] = seg(_C_GATE, _C_END)


def _project(x2, w, b, seq=None):
    t, d = x2.shape
    tm = min(ROW_TILE, t)
    row = lambda i: (i, 0)
    fixed = lambda i: (0, 0)
    widths = [(512, BF16)] * 5 + [(512, F32)] * 2 + [(512, BF16)] * 2 + [(1024, BF16)] * 2 + [(128, F32)]
    out_specs = [pl.BlockSpec((tm, n), row) for n, _ in widths]
    out_shape = [jax.ShapeDtypeStruct((t, n), dt) for n, dt in widths]
    if seq is not None:
        per_seq = seq // tm
        out_specs[5] = pl.BlockSpec((1, widths[5][0], tm), lambda i: (i // per_seq, 0, i % per_seq))
        out_shape[5] = jax.ShapeDtypeStruct((t // seq, widths[5][0], seq), F32)
    return pl.pallas_call(
        functools.partial(_proj_kernel, k_by_position=seq is not None),
        grid=(t // tm,),
        in_specs=[pl.BlockSpec((tm, d), row), pl.BlockSpec(w.shape, fixed), pl.BlockSpec(b.shape, fixed)],
        out_specs=out_specs,
        out_shape=out_shape,
        compiler_params=_cparams("parallel"),
        name="project_in",
    )(x2, w, b)


def _head_ln_gate(hh, g_row, so):
    mu = jnp.mean(hh, axis=-1, keepdims=True)
    var = jnp.mean(jnp.square(hh - mu), axis=-1, keepdims=True)
    return ((hh - mu) * lax.rsqrt(var + LN_EPS) * g_row * so.astype(F32)).astype(BF16)


def _mlstm_prompt_kernel(q_ref, k_ref, v_ref, so_ref, g_ref, ng_ref, ya_ref, c_ref, n_ref, m_ref):
    @pl.when(pl.program_id(1) == 0)
    def _():
        c_ref[...] = jnp.zeros_like(c_ref)
        n_ref[...] = jnp.zeros_like(n_ref)
        m_ref[...] = jnp.zeros_like(m_ref)

    L = q_ref.shape[0]
    gate = g_ref[...]
    cum = jax.nn.log_sigmoid(gate)
    row = lax.broadcasted_iota(I32, cum.shape, 0)
    sh = 1
    while sh < L:
        cum = cum + jnp.where(row >= sh, pltpu.roll(cum, sh, axis=0), 0.0)
        sh *= 2
    cum_t = cum.T
    gate_t = gate.T
    causal = lax.broadcasted_iota(I32, (L, L), 1) <= lax.broadcasted_iota(I32, (L, L), 0)
    lane = lax.broadcasted_iota(I32, (1, V7X_LANES), 1)
    m_all = m_ref[0]
    m_next = m_all
    for h in range(M_HEADS):
        hs = slice(h * M_DK, (h + 1) * M_DK)
        b_col = cum[:, M_HEADS + h:M_HEADS + h + 1]
        b_row = cum_t[M_HEADS + h:M_HEADS + h + 1, :]
        i_col = gate[:, h:h + 1]
        i_row = gate_t[h:h + 1, :]
        m0 = m_all[:, h:h + 1]
        dmat = jnp.where(causal, b_col - b_row + i_row, NEG_INF)
        inter = b_col + m0
        m = jnp.maximum(inter, jnp.max(dmat, axis=1, keepdims=True))
        w = jnp.exp(dmat - m)
        a = jnp.exp(inter - m)
        qh, kh, vh = q_ref[:, hs], k_ref[:, hs], v_ref[:, hs]
        c0 = c_ref[0, h]
        n0 = n_ref[0, h:h + 1, :]
        qk = _nt_dot(qh, kh) * w
        num = _dot(qk.astype(BF16), vh) + a * _dot(qh, c0.astype(BF16))
        den = jnp.sum(qk, axis=1, keepdims=True) + a * jnp.sum(qh.astype(F32) * n0, axis=1, keepdims=True)
        hh = num / jnp.maximum(jnp.abs(den), jnp.exp(-m))
        ya_ref[:, hs] = _head_ln_gate(hh, ng_ref[:, hs], so_ref[:, hs])
        m_last = m[L - 1:L, :]
        b_last = b_col[L - 1:L, :]
        w_last = jnp.exp(b_last - b_col + i_col - m_last)
        a_last = jnp.exp(b_last + m0 - m_last)
        kw = kh.astype(F32) * w_last
        c_ref[0, h] = a_last * c0 + _dot(kw.T.astype(BF16), vh)
        n_ref[0, h:h + 1, :] = a_last * n0 + jnp.sum(kw, axis=0, keepdims=True)
        m_next = jnp.where(lane == h, m_last, m_next)
    m_ref[0] = m_next


def _mlstm_prompt(mq, mk, mv, so, gates, ng, batch, seq):
    L = min(MLSTM_CHUNK, seq)
    nc = seq // L
    tok = lambda b, c: (b * nc + c, 0)
    wide = pl.BlockSpec((L, M_HEADS * M_DK), tok)
    return pl.pallas_call(
        _mlstm_prompt_kernel,
        grid=(batch, nc),
        in_specs=[wide, wide, wide, wide, pl.BlockSpec((L, V7X_LANES), tok),
                  pl.BlockSpec((1, M_HEADS * M_DV), lambda b, c: (0, 0))],
        out_specs=[wide,
                   pl.BlockSpec((1, M_HEADS, M_DK, M_DV), lambda b, c: (b, 0, 0, 0)),
                   pl.BlockSpec((1, M_HEADS, M_DK), lambda b, c: (b, 0, 0)),
                   pl.BlockSpec((1, 1, V7X_LANES), lambda b, c: (b, 0, 0))],
        out_shape=[jax.ShapeDtypeStruct((batch * seq, M_HEADS * M_DV), BF16),
                   jax.ShapeDtypeStruct((batch, M_HEADS, M_DK, M_DV), F32),
                   jax.ShapeDtypeStruct((batch, M_HEADS, M_DK), F32),
                   jax.ShapeDtypeStruct((batch, 1, V7X_LANES), F32)],
        compiler_params=_cparams("parallel", "arbitrary"),
        name="mlstm_prompt",
    )(mq, mk, mv, so, gates, ng)


def _mlstm_step_kernel(q_ref, k_ref, v_ref, so_ref, g_ref, ng_ref, c0_ref, n0_ref, m0_ref,
                       ya_ref, c_ref, n_ref, m_ref):
    gate = g_ref[0]
    m_all = m0_ref[0]
    lane = lax.broadcasted_iota(I32, (1, V7X_LANES), 1)
    diag = lax.broadcasted_iota(I32, (M_DK, M_DK), 0) == lax.broadcasted_iota(I32, (M_DK, M_DK), 1)
    m_next = m_all
    for h in range(M_HEADS):
        hs = slice(h * M_DK, (h + 1) * M_DK)
        q = q_ref[0][:, hs].astype(F32)
        k = k_ref[0][:, hs].astype(F32)
        v = v_ref[0][:, hs].astype(F32)
        ig = gate[:, h:h + 1]
        lf = jax.nn.log_sigmoid(gate[:, M_HEADS + h:M_HEADS + h + 1])
        m0 = m_all[:, h:h + 1]
        inter = lf + m0
        m = jnp.maximum(inter, ig)
        w = jnp.exp(ig - m)
        a = jnp.exp(inter - m)
        c0 = c0_ref[0, h]
        n0 = n0_ref[0, h:h + 1, :]
        qk = jnp.sum(q * k, axis=1, keepdims=True) * w
        q8 = jnp.broadcast_to(q, (V7X_SUBLANES, M_DK)).astype(BF16)
        qc = _dot(q8, c0.astype(BF16))[0:1, :]
        num = qk * v + a * qc
        den = qk + a * jnp.sum(q * n0, axis=1, keepdims=True)
        hh = num / jnp.maximum(jnp.abs(den), jnp.exp(-m))
        ya_ref[0, :, hs] = _head_ln_gate(hh, ng_ref[:, hs], so_ref[0][:, hs])
        kd = jnp.where(diag, jnp.broadcast_to(k, (M_DK, M_DK)), 0.0).astype(BF16)
        vb = jnp.broadcast_to(v, (M_DK, M_DV)).astype(BF16)
        c_ref[0, h] = a * c0 + w * _dot(kd, vb)
        n_ref[0, h:h + 1, :] = a * n0 + w * k
        m_next = jnp.where(lane == h, m, m_next)
    m_ref[0] = m_next


def _mlstm_step(mq, mk, mv, so, gates, ng, c0, n0, m0):
    db = mq.shape[0]
    r3 = lambda a: a.reshape(db, 1, a.shape[-1])
    tok = lambda b: (b, 0, 0)
    wide = pl.BlockSpec((1, 1, M_HEADS * M_DK), tok)
    narrow = pl.BlockSpec((1, 1, V7X_LANES), tok)
    cspec = pl.BlockSpec((1, M_HEADS, M_DK, M_DV), lambda b: (b, 0, 0, 0))
    nspec = pl.BlockSpec((1, M_HEADS, M_DK), tok)
    return pl.pallas_call(
        _mlstm_step_kernel,
        grid=(db,),
        in_specs=[wide, wide, wide, wide, narrow, pl.BlockSpec((1, M_HEADS * M_DV), lambda b: (0, 0)),
                  cspec, nspec, narrow],
        out_specs=[wide, cspec, nspec, narrow],
        out_shape=[jax.ShapeDtypeStruct((db, 1, M_HEADS * M_DV), BF16),
                   jax.ShapeDtypeStruct(c0.shape, F32),
                   jax.ShapeDtypeStruct(n0.shape, F32),
                   jax.ShapeDtypeStruct((db, 1, V7X_LANES), F32)],
        compiler_params=_cparams("parallel"),
        name="mlstm_step",
    )(r3(mq), r3(mk), r3(mv), r3(so), r3(gates), ng, c0, n0, m0)


def _lambda_value(lq1, lk1, lq2, lk2, lam_init):
    s1 = jnp.sum(lq1[...] * lk1[...], axis=1, keepdims=True)
    s2 = jnp.sum(lq2[...] * lk2[...], axis=1, keepdims=True)
    return jnp.exp(s1) - jnp.exp(s2) + lam_init


def _rms_gain(o, g_row, lam_init):
    return o * lax.rsqrt(jnp.mean(jnp.square(o), axis=-1, keepdims=True) + LN_EPS) * g_row * (1.0 - lam_init)


def _attn_prompt_kernel(q_ref, k_ref, v_ref, slope_ref, ng_ref, lq1, lk1, lq2, lk2, o_ref, *, lam_init):
    tq = q_ref.shape[0]
    qi = pl.program_id(2)
    q = q_ref[...]
    lane = lax.broadcasted_iota(I32, q.shape, 1)
    zero = jnp.zeros_like(q)
    qq = jnp.concatenate([jnp.where(lane < A_DK, q, zero), jnp.where(lane >= A_DK, q, zero)], axis=0)
    slope = slope_ref[0]
    slope_row = jnp.concatenate([slope] * (tq // V7X_LANES), axis=1)
    col = lax.broadcasted_iota(I32, (1, tq), 1)

    def block(j, carry, masked):
        m, l, acc = carry
        start = pl.multiple_of(j * tq, tq)
        kb = k_ref[pl.ds(start, tq), :]
        vb = v_ref[pl.ds(start, tq), :]
        s = _nt_dot(qq, kb) + slope_row * (j * tq + col).astype(F32)
        if masked:
            r = lax.broadcasted_iota(I32, s.shape, 0)
            r = jnp.where(r >= tq, r - tq, r)
            s = jnp.where(lax.broadcasted_iota(I32, s.shape, 1) <= r, s, NEG_INF)
        m_new = jnp.maximum(m, jnp.max(s, axis=1, keepdims=True))
        p = jnp.exp(s - m_new)
        alpha = jnp.exp(m - m_new)
        l = alpha * l + jnp.sum(p, axis=1, keepdims=True)
        acc = alpha * acc + _dot(p.astype(BF16), vb)
        return m_new, l, acc

    init = (jnp.full((2 * tq, 1), NEG_INF, F32), jnp.zeros((2 * tq, 1), F32), jnp.zeros((2 * tq, A_DV), F32))
    carry = lax.fori_loop(0, qi, lambda j, c: block(j, c, False), init)
    _, l, acc = block(qi, carry, True)
    lam = _lambda_value(lq1, lk1, lq2, lk2, lam_init)
    o = acc[:tq] / l[:tq] - lam * (acc[tq:] / l[tq:])
    o_ref[...] = _rms_gain(o, ng_ref[...], lam_init).astype(BF16)


def _attn_prompt(aq, akb, avb, slopes, ng, lams, batch, seq, lam_init):
    tq = min(ATTN_BLOCK, seq)
    nq = seq // tq
    hw = 2 * A_DK
    lam_spec = pl.BlockSpec((1, A_DK), lambda b, h, i: (0, 0))
    return pl.pallas_call(
        functools.partial(_attn_prompt_kernel, lam_init=lam_init),
        grid=(batch, A_HEADS, nq),
        in_specs=[pl.BlockSpec((tq, hw), lambda b, h, i: (b * nq + i, h)),
                  pl.BlockSpec((seq, hw), lambda b, h, i: (b, h)),
                  pl.BlockSpec((seq, A_DV), lambda b, h, i: (b, h)),
                  pl.BlockSpec((1, 1, V7X_LANES), lambda b, h, i: (h, 0, 0)),
                  pl.BlockSpec((1, A_DV), lambda b, h, i: (0, h)),
                  lam_spec, lam_spec, lam_spec, lam_spec],
        out_specs=pl.BlockSpec((tq, A_DV), lambda b, h, i: (b * nq + i, h)),
        out_shape=jax.ShapeDtypeStruct((batch * seq, A_HEADS * A_DV), BF16),
        compiler_params=_cparams("parallel", "parallel", "arbitrary"),
        name="diff_attn_prompt",
    )(aq, akb, avb, slopes, ng, *lams)


def _attn_decode_kernel(pt_ref, q_ref, ks_ref, vs_ref, slope_ref, ng_ref, lq1, lk1, lq2, lk2, *rest,
                        lam_init, pps):
    k_refs, v_refs = rest[:pps], rest[pps:2 * pps]
    o_ref, m_ref, l_ref, acc_ref = rest[2 * pps:]
    j = pl.program_id(1)
    nmap = 2 * A_HEADS
    width = A_HEADS * 2 * A_DK

    @pl.when(j == 0)
    def _():
        m_ref[...] = jnp.full(m_ref.shape, NEG_INF, F32)
        l_ref[...] = jnp.zeros_like(l_ref)
        acc_ref[...] = jnp.zeros_like(acc_ref)

    r8 = lax.broadcasted_iota(I32, (nmap, width), 0)
    c8 = lax.broadcasted_iota(I32, (nmap, width), 1)
    own = (c8 // A_DK) == r8
    qmat = jnp.where(own, jnp.broadcast_to(q_ref[0].astype(F32), (nmap, width)), 0.0)
    qb = qmat.astype(BF16)
    slope = slope_ref[...]
    lane = lax.broadcasted_iota(I32, (nmap, pps * PAGE_SIZE), 1)

    def fold(s):
        m_old = m_ref[...]
        m_new = jnp.maximum(m_old, jnp.max(s, axis=1, keepdims=True))
        alpha = jnp.exp(m_old - m_new)
        p = jnp.exp(s - m_new[:, 0:1])
        m_ref[...] = m_new
        return alpha, p

    q_maps = [jnp.broadcast_to(qb[r:r + 1, r * A_DK:(r + 1) * A_DK], (V7X_SUBLANES, A_DK)) for r in range(nmap)]
    s = jnp.concatenate(
        [jnp.concatenate([_dot(q_maps[r], k_refs[i][0, r // 2, r % 2].astype(BF16))[0:1] for r in range(nmap)],
                         axis=0) for i in range(pps)], axis=1)
    s = s + slope[:, 0:1] * (j * (pps * PAGE_SIZE) + lane).astype(F32)
    alpha, p = fold(s)
    l_ref[...] = alpha * l_ref[...] + jnp.sum(p, axis=1, keepdims=True)
    pv = []
    for h in range(A_HEADS):
        parts = [_dot(p[:, i * PAGE_SIZE:(i + 1) * PAGE_SIZE], v_refs[i][0, pl.ds(h, PAGE_SIZE, stride=A_HEADS), :])
                 for i in range(pps)]
        while len(parts) > 1:
            parts = [a + b for a, b in zip(parts[0::2], parts[1::2])]
        pv.append(parts[0])
    acc_ref[...] = alpha[:, 0:1] * acc_ref[...] + jnp.concatenate(pv, axis=1)

    @pl.when(j == pl.num_programs(1) - 1)
    def _():
        past = jnp.asarray(pl.num_programs(1) * pps * PAGE_SIZE, F32)
        k_new = ks_ref[0].astype(F32)
        v_new = vs_ref[0].astype(F32)
        s = jnp.sum(qmat * k_new, axis=1, keepdims=True) + slope[:, 0:1] * past
        alpha, p = fold(s)
        l = alpha * l_ref[...] + p
        acc = alpha[:, 0:1] * acc_ref[...] + p * v_new
        o = acc / l[:, 0:1]
        lam = _lambda_value(lq1, lk1, lq2, lk2, lam_init)
        d = o - lam * pltpu.roll(o, nmap - 1, axis=0)
        keep = r8 == 2 * (c8 // A_DV)
        hb = jnp.sum(jnp.where(keep, d, 0.0), axis=0, keepdims=True)
        for h in range(A_HEADS):
            hs = slice(h * A_DV, (h + 1) * A_DV)
            o_ref[0, :, hs] = _rms_gain(hb[:, hs], ng_ref[:, hs], lam_init).astype(BF16)


def _attn_decode(page_table, aq, akb, avb, cache_k, cache_v, slopes8, ng, lams, lam_init):
    db, n_pages = page_table.shape
    pps = math.gcd(PAGES_PER_STEP, n_pages)
    width = A_HEADS * 2 * A_DK
    r3 = lambda a: a.reshape(db, 1, a.shape[-1])
    tok = lambda b, j, pt: (b, 0, 0)
    fixed = lambda b, j, pt: (0, 0)
    cache_k = jnp.transpose(cache_k, (0, 2, 3, 4, 1))
    k_specs = [pl.BlockSpec((1,) + cache_k.shape[1:], lambda b, j, pt, i=i: (pt[b, j * pps + i], 0, 0, 0, 0))
               for i in range(pps)]
    cache_v = cache_v.reshape(cache_v.shape[0], PAGE_SIZE * A_HEADS, A_DV)
    v_specs = [pl.BlockSpec((1,) + cache_v.shape[1:], lambda b, j, pt, i=i: (pt[b, j * pps + i], 0, 0))
               for i in range(pps)]
    lam_spec = pl.BlockSpec((1, A_DK), fixed)
    grid_spec = pltpu.PrefetchScalarGridSpec(
        num_scalar_prefetch=1,
        grid=(db, n_pages // pps),
        in_specs=[pl.BlockSpec((1, 1, width), tok), pl.BlockSpec((1, 1, width), tok),
                  pl.BlockSpec((1, 1, width), tok),
                  pl.BlockSpec((2 * A_HEADS, V7X_LANES), fixed), pl.BlockSpec((1, width), fixed),
                  lam_spec, lam_spec, lam_spec, lam_spec] + k_specs + v_specs,
        out_specs=pl.BlockSpec((1, 1, width), tok),
        scratch_shapes=[pltpu.VMEM((2 * A_HEADS, V7X_LANES), F32), pltpu.VMEM((2 * A_HEADS, V7X_LANES), F32),
                        pltpu.VMEM((2 * A_HEADS, width), F32)],
    )
    out = pl.pallas_call(
        functools.partial(_attn_decode_kernel, lam_init=lam_init, pps=pps),
        grid_spec=grid_spec,
        out_shape=jax.ShapeDtypeStruct((db, 1, width), BF16),
        compiler_params=_cparams("parallel", "arbitrary"),
        name="diff_attn_decode",
    )(page_table, r3(aq), r3(akb), r3(avb), slopes8, ng, *lams, *([cache_k] * pps), *([cache_v] * pps))
    return out.reshape(db, width)


def _layer_norm(x, g, b):
    mu = jnp.mean(x, axis=-1, keepdims=True)
    var = jnp.mean(jnp.square(x - mu), axis=-1, keepdims=True)
    return (x - mu) * lax.rsqrt(var + LN_EPS) * g + b


def _merge_kernel(x_ref, ya_ref, yb_ref, sga_ref, sgb_ref, p_ref, wa_ref, wb_ref, wo_ref, g1_ref, b1_ref,
                  wg_ref, wp_ref, wq_ref, x1_ref, base_ref, q_ref, *, alpha):
    ya = _dot(ya_ref[...], wa_ref[...])
    yb = _dot(yb_ref[...], wb_ref[...])
    y = sga_ref[...].astype(F32) * ya + sgb_ref[...].astype(F32) * yb
    y = _dot(y.astype(BF16), wo_ref[...])
    x1 = _layer_norm(alpha * x_ref[...] + y, g1_ref[...], b1_ref[...])
    x1_ref[...] = x1
    x1b = x1.astype(BF16)
    pe = jax.nn.sigmoid(_dot(x1b, wg_ref[...])) * _dot(p_ref[...].astype(BF16), wp_ref[...])
    base_ref[...] = alpha * x1 + pe
    q_ref[...] = _dot(x1b, wq_ref[...]).astype(BF16)


def _merge(x2, ya, yb, sga, sgb, p2, wa, wb, wo, g1, b1, wg, wp, wq, alpha):
    t, d = x2.shape
    tm = min(ROW_TILE, t)
    row = lambda i: (i, 0)
    fixed = lambda i: (0, 0)
    rows = lambda a: pl.BlockSpec((tm, a.shape[1]), row)
    full = lambda a: pl.BlockSpec(a.shape, fixed)
    nq = wq.shape[1]
    return pl.pallas_call(
        functools.partial(_merge_kernel, alpha=alpha),
        grid=(t // tm,),
        in_specs=[rows(x2), rows(ya), rows(yb), rows(sga), rows(sgb), rows(p2),
                  full(wa), full(wb), full(wo), full(g1), full(b1), full(wg), full(wp), full(wq)],
        out_specs=[pl.BlockSpec((tm, d), row), pl.BlockSpec((tm, d), row), pl.BlockSpec((tm, nq), row)],
        out_shape=[jax.ShapeDtypeStruct((t, d), F32), jax.ShapeDtypeStruct((t, d), F32),
                   jax.ShapeDtypeStruct((t, nq), BF16)],
        compiler_params=_cparams("parallel"),
        name="merge_ln1",
    )(x2, ya, yb, sga, sgb, p2, wa, wb, wo, g1, b1, wg, wp, wq)


def _topk_rows(s, k, payload=None):
    n = s.shape[0]
    rows = lax.broadcasted_iota(I32, s.shape, 0).astype(F32)
    vals, picks = [], []
    for _ in range(k):
        mx = jnp.max(s, axis=0, keepdims=True)
        idx = jnp.min(jnp.where(s == mx, rows, float(n)), axis=0, keepdims=True)
        sel = rows == idx
        if payload is None:
            picks.append(idx)
        else:
            picks.append(jnp.sum(jnp.where(sel, payload, 0.0), axis=0, keepdims=True))
        vals.append(mx)
        s = jnp.where(sel, NEG_INF, s)
    return jnp.concatenate(vals, axis=0), jnp.concatenate(picks, axis=0)


def _pair_candidates(sv, si):
    sub = lax.broadcasted_iota(I32, (V7X_SUBLANES,) + sv[0].shape[1:], 0)
    pair = lambda x, y, mul: x * mul + y
    vals = [sv[0][0:1] + sv[1], sv[0][1:2] + sv[1][0:V7X_SUBLANES]]
    idxs = [pair(si[0][0:1], si[1], float(P_NKEYS)), pair(si[0][1:2], si[1][0:V7X_SUBLANES], float(P_NKEYS))]
    for a in range(2, V7X_SUBLANES):
        keep = sub < P_TOPK // (a + 1)
        vals.append(jnp.where(keep, sv[0][a:a + 1] + sv[1][0:V7X_SUBLANES], NEG_INF))
        idxs.append(pair(si[0][a:a + 1], si[1][0:V7X_SUBLANES], float(P_NKEYS)))
    vals.append(sv[0][V7X_SUBLANES:] + sv[1][0:1])
    idxs.append(pair(si[0][V7X_SUBLANES:], si[1][0:1], float(P_NKEYS)))
    return jnp.concatenate(vals, axis=0), jnp.concatenate(idxs, axis=0)


def _route_kernel(q_ref, sk_ref, idx_ref, g_ref, et_ref, gt_ref):
    h = pl.program_id(1)
    rows = pl.ds(pl.multiple_of(h * P_TOPK, P_TOPK), P_TOPK)
    for part in range(q_ref.shape[0] // V7X_LANES):
        ts = slice(part * V7X_LANES, (part + 1) * V7X_LANES)
        sv, si = [], []
        for c in range(2):
            s = _nt_dot(sk_ref[0, c], q_ref[ts, c * P_DHALF:(c + 1) * P_DHALF])
            v, i = _topk_rows(s, P_TOPK)
            sv.append(v)
            si.append(i)
        cand, cidx = _pair_candidates(sv, si)
        cv, eidx = _topk_rows(cand, P_TOPK, payload=cidx)
        e = jnp.exp(cv - cv[0:1])
        et_ref[rows, ts] = eidx
        gt_ref[rows, ts] = e / jnp.sum(e, axis=0, keepdims=True)

    @pl.when(h == P_HEADS - 1)
    def _():
        idx_ref[...] = et_ref[...].T.astype(I32)
        g_ref[...] = gt_ref[...].T


def _route(q, subkeys, tm):
    t = q.shape[0]
    out = pl.BlockSpec((tm, N_SEL), lambda i, h: (i, 0))
    return pl.pallas_call(
        _route_kernel,
        grid=(t // tm, P_HEADS),
        in_specs=[pl.BlockSpec((tm, 2 * P_DHALF), lambda i, h: (i, h)),
                  pl.BlockSpec((1, 2, P_NKEYS, P_DHALF), lambda i, h: (h, 0, 0, 0))],
        out_specs=[out, out],
        out_shape=[jax.ShapeDtypeStruct((t, N_SEL), I32), jax.ShapeDtypeStruct((t, N_SEL), F32)],
        scratch_shapes=[pltpu.VMEM((N_SEL, tm), F32), pltpu.VMEM((N_SEL, tm), F32)],
        compiler_params=_cparams("parallel", "arbitrary"),
        name="peer_route",
    )(q, subkeys)


def _load_row(tab_ref, e):
    return tab_ref[e].astype(F32)


def _sublane_merge(x, y, sh, mask):
    c = jnp.where(mask, x, y)
    d = jnp.where(mask, y, x)
    if 2 * sh == V7X_SUBLANES:
        return c + pltpu.roll(d, sh, axis=0)
    return c + jnp.where(mask, pltpu.roll(d, V7X_SUBLANES - sh, axis=0), pltpu.roll(d, sh, axis=0))


def _stage_blocks(i, nblk, tab_hbm, tab_vmem, per_block, sems):
    slot = i % 2

    def copies(blk, sl):
        return [pltpu.make_async_copy(h.at[blk], s.at[pl.ds(sl * h.shape[1], h.shape[1])],
                                      sems.at[1 + 2 * n + sl])
                for n, (h, s) in enumerate(per_block)]

    @pl.when(i == 0)
    def _():
        table = pltpu.make_async_copy(tab_hbm, tab_vmem, sems.at[0])
        table.start()
        for c in copies(0, 0):
            c.start()
        table.wait()

    for c in copies(i, slot):
        c.wait()

    @pl.when(i + 1 < nblk)
    def _():
        for c in copies(i + 1, 1 - slot):
            c.start()

    return slot


def _peer_up_kernel(x_ref, g_ref, idx_hbm, tab_hbm, w_ref, tab_vmem, idx_smem, part_ref, sems):
    i = pl.program_id(0)
    slot = _stage_blocks(i, pl.num_programs(0), tab_hbm, tab_vmem, [(idx_hbm, idx_smem)], sems)
    tb = x_ref.shape[0]
    chunk = part_ref.shape[0] // N_SEL
    sub = lax.broadcasted_iota(I32, (V7X_SUBLANES, V7X_LANES), 0)
    masks = {sh: (sub & sh) == 0 for sh in (4, 2, 1)}
    ones = jnp.ones((V7X_LANES, V7X_LANES), BF16)
    diag = (lax.broadcasted_iota(I32, (1, N_SEL, V7X_LANES), 1)
            == lax.broadcasted_iota(I32, (1, N_SEL, V7X_LANES), 2))

    def token_chunk(c, carry):
        def token(tt, carry):
            t = c * chunk + tt
            xt = x_ref[t]
            tok_base = (slot * tb + t) * N_SEL

            def group(j, carry):
                col = pl.multiple_of(j * PEER_GROUP, PEER_GROUP)
                for k in range(PEER_GROUP // V7X_SUBLANES):
                    first = col + k * V7X_SUBLANES
                    parts = [_load_row(tab_vmem, idx_smem[tok_base + first + r]) * xt
                             for r in range(V7X_SUBLANES)]
                    for sh in (4, 2, 1):
                        parts = [_sublane_merge(parts[a], parts[a + sh], sh, masks[sh])
                                 for a in range(len(parts)) if (a & sh) == 0]
                    part_ref[pl.ds(pl.multiple_of(tt * N_SEL + first, V7X_SUBLANES), V7X_SUBLANES), :] = parts[0]
                return carry

            return lax.fori_loop(0, N_SEL // PEER_GROUP, group, carry)

        lax.fori_loop(0, chunk, token, 0)
        part = part_ref[...]
        hi = part.astype(BF16)
        lo = (part - hi.astype(F32)).astype(BF16)
        tot = (_dot(hi, ones) + _dot(lo, ones)).reshape(chunk, N_SEL, V7X_LANES)
        s = jnp.sum(jnp.where(diag, tot, 0.0), axis=1)
        act = 0.5 * s * (1.0 + lax.erf(s * (2.0 ** -0.5)))
        rows = pl.ds(pl.multiple_of(c * chunk, chunk), chunk)
        w_ref[rows, :] = g_ref[rows, :] * act
        return carry

    lax.fori_loop(0, tb // chunk, token_chunk, 0)


def _peer_down_kernel(idx_hbm, w_hbm, tab_hbm, f_ref, tab_vmem, idx_smem, w_smem, sems):
    i = pl.program_id(0)
    slot = _stage_blocks(i, pl.num_programs(0), tab_hbm, tab_vmem,
                         [(idx_hbm, idx_smem), (w_hbm, w_smem)], sems)
    tb = f_ref.shape[0]
    n_acc = 4
    zero = jnp.zeros((V7X_SUBLANES, V7X_LANES), F32)

    def token(t, carry):
        tok_base = (slot * tb + t) * N_SEL

        def group(j, accs):
            base = tok_base + j * PEER_GROUP_DOWN
            accs = list(accs)
            for r in range(PEER_GROUP_DOWN):
                row = _load_row(tab_vmem, idx_smem[base + r])
                accs[r % n_acc] = accs[r % n_acc] + w_smem[base + r] * row
            return tuple(accs)

        accs = lax.fori_loop(0, N_SEL // PEER_GROUP_DOWN, group, (zero,) * n_acc)
        f_ref[t] = (accs[0] + accs[1]) + (accs[2] + accs[3])
        return carry

    lax.fori_loop(0, tb, token, 0)


def _peer_experts(x1, g, idx, u_tab, v_tab, tb):
    t, d = x1.shape
    nblk = t // tb
    x3 = x1.reshape(t, V7X_SUBLANES, d // V7X_SUBLANES)
    idx = idx.reshape(nblk, tb * N_SEL)
    any_spec = pl.BlockSpec(memory_space=pl.ANY)
    w = pl.pallas_call(
        _peer_up_kernel,
        grid=(nblk,),
        in_specs=[pl.BlockSpec((tb,) + x3.shape[1:], lambda i: (i, 0, 0)),
                  pl.BlockSpec((tb, N_SEL), lambda i: (i, 0)), any_spec, any_spec],
        out_specs=pl.BlockSpec((tb, N_SEL), lambda i: (i, 0)),
        out_shape=jax.ShapeDtypeStruct((t, N_SEL), F32),
        scratch_shapes=[pltpu.VMEM(u_tab.shape, u_tab.dtype), pltpu.SMEM((2 * tb * N_SEL,), I32),
                        pltpu.VMEM((PEER_CHUNK * N_SEL, V7X_LANES), F32), pltpu.SemaphoreType.DMA((3,))],
        compiler_params=_cparams("arbitrary"),
        name="peer_up",
    )(x3, g, idx, u_tab)
    f3 = pl.pallas_call(
        _peer_down_kernel,
        grid=(nblk,),
        in_specs=[any_spec, any_spec, any_spec],
        out_specs=pl.BlockSpec((tb,) + x3.shape[1:], lambda i: (i, 0, 0)),
        out_shape=jax.ShapeDtypeStruct(x3.shape, F32),
        scratch_shapes=[pltpu.VMEM(v_tab.shape, v_tab.dtype), pltpu.SMEM((2 * tb * N_SEL,), I32),
                        pltpu.SMEM((2 * tb * N_SEL,), F32), pltpu.SemaphoreType.DMA((5,))],
        compiler_params=_cparams("arbitrary"),
        name="peer_down",
    )(idx, w.reshape(nblk, tb * N_SEL), v_tab)
    return f3.reshape(t, d)


def _final_kernel(base_ref, f_ref, g_ref, b_ref, o_ref):
    o_ref[...] = _layer_norm(base_ref[...] + f_ref[...], g_ref[...], b_ref[...])


def _final(base, f, g2, b2):
    t, d = base.shape
    tm = min(ROW_TILE, t)
    row = pl.BlockSpec((tm, d), lambda i: (i, 0))
    vec = pl.BlockSpec((1, d), lambda i: (0, 0))
    return pl.pallas_call(
        _final_kernel,
        grid=(t // tm,),
        in_specs=[row, row, vec, vec],
        out_specs=row,
        out_shape=jax.ShapeDtypeStruct((t, d), F32),
        compiler_params=_cparams("parallel"),
        name="residual_ln2",
    )(base, f, g2, b2)


def _reorder_in_proj(w, b):
    hk = M_HEADS * M_DK
    widths = (hk, hk, M_HEADS * M_DV, M_HEADS * M_DV, M_HEADS, M_HEADS,
              A_HEADS * 2 * A_DK, A_HEADS * 2 * A_DK, A_HEADS * A_DV)
    cuts = [0]
    for n in widths:
        cuts.append(cuts[-1] + n)
    total = w.shape[1]
    d_model = (total - cuts[-1]) // 2
    cuts += [cuts[-1] + d_model, total]
    piece = lambda a, i: a[..., cuts[i]:cuts[i + 1]]
    order = (0, 1, 2, 3, 6, 7, 8, 9, 10, 4, 5)
    pad = V7X_LANES - 2 * M_HEADS
    wr = jnp.concatenate([piece(w, i) for i in order] + [jnp.zeros((w.shape[0], pad), w.dtype)], axis=1)
    br = jnp.concatenate([piece(b, i) for i in order] + [jnp.zeros((pad,), b.dtype)], axis=0)
    return wr.astype(BF16), br.reshape(1, -1)


def _table_tiles(tab):
    n, d = tab.shape
    return tab.astype(BF16).reshape(n, V7X_SUBLANES, d // V7X_SUBLANES)


def _token_local(x2, ya, yb, sga, sgb, p2, lw, alpha):
    t = x2.shape[0]
    if t % PEER_TILE:
        pad = lambda a: jnp.pad(a, ((0, PEER_TILE - t % PEER_TILE), (0, 0)))
        return _token_local(pad(x2), pad(ya), pad(yb), pad(sga), pad(sgb), pad(p2), lw, alpha)[:t]
    x1, base, q = _merge(x2, ya, yb, sga, sgb, p2, lw["wa"], lw["wb"], lw["wo"], lw["g1"], lw["b1"],
                         lw["wg"], lw["wp"], lw["wq"], alpha)
    idx, g = _route(q, lw["subkeys"], ROUTE_TILE if t % ROUTE_TILE == 0 else PEER_TILE)
    f = _peer_experts(x1, g, idx, lw["u_tab"], lw["v_tab"], PEER_TILE)
    return _final(base, f, lw["g2"], lw["b2"])


def kernel(x_prompt, x_sample, cache_k, cache_v, state_C, state_n, state_m, page_table, p_prompt, p_sample,
           w_in, b_in, lambda_q1, lambda_k1, lambda_q2, lambda_k2, mlstm_norm_g, diff_norm_g, w_branch_a,
           w_branch_b, w_out, ln1_g, ln1_b, peer_wq, peer_subkeys, peer_u, peer_v, ple_w_gate, ple_w_proj,
           ln2_g, ln2_b):
    depth = w_in.shape[0]
    batch, seq, d_model = x_prompt.shape
    db, dseq, _ = x_sample.shape
    assert dseq == 1, "the decode path handles one new token per request"
    alpha = (2.0 * depth) ** 0.25
    slopes = 2.0 ** (-8.0 * (jnp.arange(A_HEADS, dtype=F32) + 1.0) / A_HEADS)
    slopes_h = jnp.broadcast_to(slopes[:, None, None], (A_HEADS, 1, V7X_LANES))
    slopes_8 = jnp.broadcast_to(jnp.repeat(slopes, 2)[:, None], (2 * A_HEADS, V7X_LANES))

    hp = x_prompt.reshape(batch * seq, d_model)
    hs = x_sample.reshape(db, d_model)
    outs = [[] for _ in range(10)]
    for l in range(depth):
        lam_init = 0.8 - 0.6 * math.exp(-0.3 * l)
        w_l, b_l = _reorder_in_proj(w_in[l], b_in[l])
        row = lambda a: a.reshape(1, -1)
        lw = dict(wa=w_branch_a[l].astype(BF16), wb=w_branch_b[l].astype(BF16), wo=w_out[l].astype(BF16),
                  g1=row(ln1_g[l]), b1=row(ln1_b[l]), wg=ple_w_gate[l].astype(BF16),
                  wp=ple_w_proj[l].astype(BF16), wq=peer_wq[l].astype(BF16),
                  subkeys=peer_subkeys[l].astype(BF16), u_tab=_table_tiles(peer_u[l]),
                  v_tab=_table_tiles(peer_v[l]), g2=row(ln2_g[l]), b2=row(ln2_b[l]))
        lams = [row(a[l].astype(F32)) for a in (lambda_q1, lambda_k1, lambda_q2, lambda_k2)]
        mng = row(mlstm_norm_g[l])
        dng = row(diff_norm_g[l])

        mq, mk, mv, so, aq, akt, avf, akb, avb, sga, sgb, gates = _project(hp, w_l, b_l, seq=seq)
        ya, c_p, n_p, m_p = _mlstm_prompt(mq, mk, mv, so, gates, mng, batch, seq)
        yb = _attn_prompt(aq, akb, avb, slopes_h, dng, lams, batch, seq, lam_init)
        hp = _token_local(hp, ya, yb, sga, sgb, p_prompt[l].reshape(batch * seq, -1), lw, alpha)
        outs[0].append(jnp.transpose(akt.reshape(batch, A_HEADS, 2, A_DK, seq), (0, 4, 1, 2, 3)))
        outs[1].append(avf.reshape(batch, seq, A_HEADS, A_DV))
        outs[2].append(c_p)
        outs[3].append(n_p)
        outs[4].append(m_p[:, 0, :M_HEADS])

        mq, mk, mv, so, aq, akf, avf, akb, avb, sga, sgb, gates = _project(hs, w_l, b_l)
        m0 = jnp.pad(state_m[l].astype(F32), ((0, 0), (0, V7X_LANES - M_HEADS))).reshape(db, 1, V7X_LANES)
        ya, c_s, n_s, m_s = _mlstm_step(mq, mk, mv, so, gates, mng, state_C[l].astype(F32),
                                        state_n[l].astype(F32), m0)
        yb = _attn_decode(page_table, aq, akb, avb, cache_k[l], cache_v[l], slopes_8, dng, lams, lam_init)
        hs = _token_local(hs, ya.reshape(db, -1), yb, sga, sgb, p_sample[l].reshape(db, -1), lw, alpha)
        outs[5].append(akf.reshape(db, 1, A_HEADS, 2, A_DK))
        outs[6].append(avf.reshape(db, 1, A_HEADS, A_DV))
        outs[7].append(c_s)
        outs[8].append(n_s)
        outs[9].append(m_s[:, 0, :M_HEADS])

    st = [jnp.stack(o, 0) for o in outs]
    return (hp.reshape(batch, seq, d_model), hs.reshape(db, 1, d_model),
            st[0], st[1], st[2], st[3], st[4], st[5], st[6], st[7], st[8], st[9])
```

```python
import functools
import math

import jax
import jax.numpy as jnp
from jax import lax
from jax.experimental import pallas as pl
from jax.experimental.pallas import tpu as pltpu

F32 = jnp.float32
BF16 = jnp.bfloat16
I32 = jnp.int32

M_HEADS, M_DK, M_DV = 4, 128, 128
A_HEADS, A_DK, A_DV = 4, 64, 128
P_HEADS, P_NKEYS, P_DHALF, P_TOPK = 8, 128, 128, 16
PAGE_SIZE = 128
LN_EPS = 1e-5
N_SEL = P_HEADS * P_TOPK

V7X_LANES = 128
V7X_SUBLANES = 8
V7X_VMEM_LIMIT = 56 * 1024 * 1024

MLSTM_CHUNK = 128
ATTN_BLOCK = 256
PAGES_PER_STEP = 8
ROW_TILE = 256
PEER_TILE = 128
PEER_GROUP = 128
PEER_CHUNK = 32
ROUTE_TILE = 256

NEG_INF = float("-inf")


def _cparams(*sem):
    return pltpu.CompilerParams(dimension_semantics=sem, vmem_limit_bytes=V7X_VMEM_LIMIT)


def _nt_dot(a, b):
    return lax.dot_general(a, b, (((1,), (1,)), ((), ())), preferred_element_type=F32)


def _dot(a, b):
    return jnp.dot(a, b, preferred_element_type=F32)


_C_MQ, _C_MK, _C_MV, _C_MO, _C_AQ, _C_AK, _C_AV, _C_GA = 0, 512, 1024, 1536, 2048, 2560, 3072, 3584
_C_GB, _C_GATE, _C_END = 4608, 5632, 5760


def _proj_kernel(x_ref, w_ref, b_ref, mq, mk, mv, so, aq, akf, avf, akb, avb, sga, sgb, gates, *, k_by_position):
    xb = x_ref[...].astype(BF16)

    def seg(lo, hi):
        return _dot(xb, w_ref[:, lo:hi]) + b_ref[:, lo:hi]

    mq[...] = seg(_C_MQ, _C_MK).astype(BF16)
    mk[...] = (seg(_C_MK, _C_MV) * (M_DK ** -0.5)).astype(BF16)
    mv[...] = seg(_C_MV, _C_MO).astype(BF16)
    so[...] = jax.nn.sigmoid(seg(_C_MO, _C_AQ)).astype(BF16)
    aq[...] = (seg(_C_AQ, _C_AK) * (A_DK ** -0.5)).astype(BF16)
    k = seg(_C_AK, _C_AV)
    if k_by_position:
        akf[0] = k.T
    else:
        akf[...] = k
    akb[...] = k.astype(BF16)
    v = seg(_C_AV, _C_GA)
    avf[...] = v
    avb[...] = v.astype(BF16)
    sga[...] = jax.nn.sigmoid(seg(_C_GA, _C_GB)).astype(BF16)
    sgb[...] = jax.nn.sigmoid(seg(_C_GB, _C_GATE)).astype(BF16)
    gates[...] = seg(_C_GATE, _C_END)


def _project(x2, w, b, seq=None):
    t, d = x2.shape
    tm = min(ROW_TILE, t)
    row = lambda i: (i, 0)
    fixed = lambda i: (0, 0)
    widths = [(512, BF16)] * 5 + [(512, F32)] * 2 + [(512, BF16)] * 2 + [(1024, BF16)] * 2 + [(128, F32)]
    out_specs = [pl.BlockSpec((tm, n), row) for n, _ in widths]
    out_shape = [jax.ShapeDtypeStruct((t, n), dt) for n, dt in widths]
    if seq is not None:
        per_seq = seq // tm
        out_specs[5] = pl.BlockSpec((1, widths[5][0], tm), lambda i: (i // per_seq, 0, i % per_seq))
        out_shape[5] = jax.ShapeDtypeStruct((t // seq, widths[5][0], seq), F32)
    return pl.pallas_call(
        functools.partial(_proj_kernel, k_by_position=seq is not None),
        grid=(t // tm,),
        in_specs=[pl.BlockSpec((tm, d), row), pl.BlockSpec(w.shape, fixed), pl.BlockSpec(b.shape, fixed)],
        out_specs=out_specs,
        out_shape=out_shape,
        compiler_params=_cparams("parallel"),
        name="project_in",
    )(x2, w, b)


def _head_ln_gate(hh, g_row, so):
    mu = jnp.mean(hh, axis=-1, keepdims=True)
    var = jnp.mean(jnp.square(hh - mu), axis=-1, keepdims=True)
    return ((hh - mu) * lax.rsqrt(var + LN_EPS) * g_row * so.astype(F32)).astype(BF16)


def _mlstm_prompt_kernel(q_ref, k_ref, v_ref, so_ref, g_ref, ng_ref, ya_ref, c_ref, n_ref, m_ref):
    @pl.when(pl.program_id(1) == 0)
    def _():
        c_ref[...] = jnp.zeros_like(c_ref)
        n_ref[...] = jnp.zeros_like(n_ref)
        m_ref[...] = jnp.zeros_like(m_ref)

    L = q_ref.shape[0]
    gate = g_ref[...]
    cum = jax.nn.log_sigmoid(gate)
    row = lax.broadcasted_iota(I32, cum.shape, 0)
    sh = 1
    while sh < L:
        cum = cum + jnp.where(row >= sh, pltpu.roll(cum, sh, axis=0), 0.0)
        sh *= 2
    cum_t = cum.T
    gate_t = gate.T
    causal = lax.broadcasted_iota(I32, (L, L), 1) <= lax.broadcasted_iota(I32, (L, L), 0)
    lane = lax.broadcasted_iota(I32, (1, V7X_LANES), 1)
    m_all = m_ref[0]
    m_next = m_all
    for h in range(M_HEADS):
        hs = slice(h * M_DK, (h + 1) * M_DK)
        b_col = cum[:, M_HEADS + h:M_HEADS + h + 1]
        b_row = cum_t[M_HEADS + h:M_HEADS + h + 1, :]
        i_col = gate[:, h:h + 1]
        i_row = gate_t[h:h + 1, :]
        m0 = m_all[:, h:h + 1]
        dmat = jnp.where(causal, b_col - b_row + i_row, NEG_INF)
        inter = b_col + m0
        m = jnp.maximum(inter, jnp.max(dmat, axis=1, keepdims=True))
        w = jnp.exp(dmat - m)
        a = jnp.exp(inter - m)
        qh, kh, vh = q_ref[:, hs], k_ref[:, hs], v_ref[:, hs]
        c0 = c_ref[0, h]
        n0 = n_ref[0, h:h + 1, :]
        qk = _nt_dot(qh, kh) * w
        num = _dot(qk.astype(BF16), vh) + a * _dot(qh, c0.astype(BF16))
        den = jnp.sum(qk, axis=1, keepdims=True) + a * jnp.sum(qh.astype(F32) * n0, axis=1, keepdims=True)
        hh = num / jnp.maximum(jnp.abs(den), jnp.exp(-m))
        ya_ref[:, hs] = _head_ln_gate(hh, ng_ref[:, hs], so_ref[:, hs])
        m_last = m[L - 1:L, :]
        b_last = b_col[L - 1:L, :]
        w_last = jnp.exp(b_last - b_col + i_col - m_last)
        a_last = jnp.exp(b_last + m0 - m_last)
        kw = kh.astype(F32) * w_last
        c_ref[0, h] = a_last * c0 + _dot(kw.T.astype(BF16), vh)
        n_ref[0, h:h + 1, :] = a_last * n0 + jnp.sum(kw, axis=0, keepdims=True)
        m_next = jnp.where(lane == h, m_last, m_next)
    m_ref[0] = m_next


def _mlstm_prompt(mq, mk, mv, so, gates, ng, batch, seq):
    L = min(MLSTM_CHUNK, seq)
    nc = seq // L
    tok = lambda b, c: (b * nc + c, 0)
    wide = pl.BlockSpec((L, M_HEADS * M_DK), tok)
    return pl.pallas_call(
        _mlstm_prompt_kernel,
        grid=(batch, nc),
        in_specs=[wide, wide, wide, wide, pl.BlockSpec((L, V7X_LANES), tok),
                  pl.BlockSpec((1, M_HEADS * M_DV), lambda b, c: (0, 0))],
        out_specs=[wide,
                   pl.BlockSpec((1, M_HEADS, M_DK, M_DV), lambda b, c: (b, 0, 0, 0)),
                   pl.BlockSpec((1, M_HEADS, M_DK), lambda b, c: (b, 0, 0)),
                   pl.BlockSpec((1, 1, V7X_LANES), lambda b, c: (b, 0, 0))],
        out_shape=[jax.ShapeDtypeStruct((batch * seq, M_HEADS * M_DV), BF16),
                   jax.ShapeDtypeStruct((batch, M_HEADS, M_DK, M_DV), F32),
                   jax.ShapeDtypeStruct((batch, M_HEADS, M_DK), F32),
                   jax.ShapeDtypeStruct((batch, 1, V7X_LANES), F32)],
        compiler_params=_cparams("parallel", "arbitrary"),
        name="mlstm_prompt",
    )(mq, mk, mv, so, gates, ng)


def _mlstm_step_kernel(q_ref, k_ref, v_ref, so_ref, g_ref, ng_ref, c0_ref, n0_ref, m0_ref,
                       ya_ref, c_ref, n_ref, m_ref):
    gate = g_ref[0]
    m_all = m0_ref[0]
    lane = lax.broadcasted_iota(I32, (1, V7X_LANES), 1)
    diag = lax.broadcasted_iota(I32, (M_DK, M_DK), 0) == lax.broadcasted_iota(I32, (M_DK, M_DK), 1)
    m_next = m_all
    for h in range(M_HEADS):
        hs = slice(h * M_DK, (h + 1) * M_DK)
        q = q_ref[0][:, hs].astype(F32)
        k = k_ref[0][:, hs].astype(F32)
        v = v_ref[0][:, hs].astype(F32)
        ig = gate[:, h:h + 1]
        lf = jax.nn.log_sigmoid(gate[:, M_HEADS + h:M_HEADS + h + 1])
        m0 = m_all[:, h:h + 1]
        inter = lf + m0
        m = jnp.maximum(inter, ig)
        w = jnp.exp(ig - m)
        a = jnp.exp(inter - m)
        c0 = c0_ref[0, h]
        n0 = n0_ref[0, h:h + 1, :]
        qk = jnp.sum(q * k, axis=1, keepdims=True) * w
        q8 = jnp.broadcast_to(q, (V7X_SUBLANES, M_DK)).astype(BF16)
        qc = _dot(q8, c0.astype(BF16))[0:1, :]
        num = qk * v + a * qc
        den = qk + a * jnp.sum(q * n0, axis=1, keepdims=True)
        hh = num / jnp.maximum(jnp.abs(den), jnp.exp(-m))
        ya_ref[0, :, hs] = _head_ln_gate(hh, ng_ref[:, hs], so_ref[0][:, hs])
        kd = jnp.where(diag, jnp.broadcast_to(k, (M_DK, M_DK)), 0.0).astype(BF16)
        vb = jnp.broadcast_to(v, (M_DK, M_DV)).astype(BF16)
        c_ref[0, h] = a * c0 + w * _dot(kd, vb)
        n_ref[0, h:h + 1, :] = a * n0 + w * k
        m_next = jnp.where(lane == h, m, m_next)
    m_ref[0] = m_next


def _mlstm_step(mq, mk, mv, so, gates, ng, c0, n0, m0):
    db = mq.shape[0]
    r3 = lambda a: a.reshape(db, 1, a.shape[-1])
    tok = lambda b: (b, 0, 0)
    wide = pl.BlockSpec((1, 1, M_HEADS * M_DK), tok)
    narrow = pl.BlockSpec((1, 1, V7X_LANES), tok)
    cspec = pl.BlockSpec((1, M_HEADS, M_DK, M_DV), lambda b: (b, 0, 0, 0))
    nspec = pl.BlockSpec((1, M_HEADS, M_DK), tok)
    return pl.pallas_call(
        _mlstm_step_kernel,
        grid=(db,),
        in_specs=[wide, wide, wide, wide, narrow, pl.BlockSpec((1, M_HEADS * M_DV), lambda b: (0, 0)),
                  cspec, nspec, narrow],
        out_specs=[wide, cspec, nspec, narrow],
        out_shape=[jax.ShapeDtypeStruct((db, 1, M_HEADS * M_DV), BF16),
                   jax.ShapeDtypeStruct(c0.shape, F32),
                   jax.ShapeDtypeStruct(n0.shape, F32),
                   jax.ShapeDtypeStruct((db, 1, V7X_LANES), F32)],
        compiler_params=_cparams("parallel"),
        name="mlstm_step",
    )(r3(mq), r3(mk), r3(mv), r3(so), r3(gates), ng, c0, n0, m0)


def _lambda_value(lq1, lk1, lq2, lk2, lam_init):
    s1 = jnp.sum(lq1[...] * lk1[...], axis=1, keepdims=True)
    s2 = jnp.sum(lq2[...] * lk2[...], axis=1, keepdims=True)
    return jnp.exp(s1) - jnp.exp(s2) + lam_init


def _rms_gain(o, g_row, lam_init):
    return o * lax.rsqrt(jnp.mean(jnp.square(o), axis=-1, keepdims=True) + LN_EPS) * g_row * (1.0 - lam_init)


def _attn_prompt_kernel(q_ref, k_ref, v_ref, slope_ref, ng_ref, lq1, lk1, lq2, lk2, o_ref, *, lam_init):
    tq = q_ref.shape[0]
    qi = pl.program_id(2)
    q = q_ref[...]
    lane = lax.broadcasted_iota(I32, q.shape, 1)
    zero = jnp.zeros_like(q)
    qq = jnp.concatenate([jnp.where(lane < A_DK, q, zero), jnp.where(lane >= A_DK, q, zero)], axis=0)
    slope = slope_ref[0]
    slope_row = jnp.concatenate([slope] * (tq // V7X_LANES), axis=1)
    col = lax.broadcasted_iota(I32, (1, tq), 1)

    def block(j, carry, masked):
        m, l, acc = carry
        start = pl.multiple_of(j * tq, tq)
        kb = k_ref[pl.ds(start, tq), :]
        vb = v_ref[pl.ds(start, tq), :]
        s = _nt_dot(qq, kb) + slope_row * (j * tq + col).astype(F32)
        if masked:
            r = lax.broadcasted_iota(I32, s.shape, 0)
            r = jnp.where(r >= tq, r - tq, r)
            s = jnp.where(lax.broadcasted_iota(I32, s.shape, 1) <= r, s, NEG_INF)
        m_new = jnp.maximum(m, jnp.max(s, axis=1, keepdims=True))
        p = jnp.exp(s - m_new)
        alpha = jnp.exp(m - m_new)
        l = alpha * l + jnp.sum(p, axis=1, keepdims=True)
        acc = alpha * acc + _dot(p.astype(BF16), vb)
        return m_new, l, acc

    init = (jnp.full((2 * tq, 1), NEG_INF, F32), jnp.zeros((2 * tq, 1), F32), jnp.zeros((2 * tq, A_DV), F32))
    carry = lax.fori_loop(0, qi, lambda j, c: block(j, c, False), init)
    _, l, acc = block(qi, carry, True)
    lam = _lambda_value(lq1, lk1, lq2, lk2, lam_init)
    o = acc[:tq] / l[:tq] - lam * (acc[tq:] / l[tq:])
    o_ref[...] = _rms_gain(o, ng_ref[...], lam_init).astype(BF16)


def _attn_prompt(aq, akb, avb, slopes, ng, lams, batch, seq, lam_init):
    tq = min(ATTN_BLOCK, seq)
    nq = seq // tq
    hw = 2 * A_DK
    lam_spec = pl.BlockSpec((1, A_DK), lambda b, h, i: (0, 0))
    return pl.pallas_call(
        functools.partial(_attn_prompt_kernel, lam_init=lam_init),
        grid=(batch, A_HEADS, nq),
        in_specs=[pl.BlockSpec((tq, hw), lambda b, h, i: (b * nq + i, h)),
                  pl.BlockSpec((seq, hw), lambda b, h, i: (b, h)),
                  pl.BlockSpec((seq, A_DV), lambda b, h, i: (b, h)),
                  pl.BlockSpec((1, 1, V7X_LANES), lambda b, h, i: (h, 0, 0)),
                  pl.BlockSpec((1, A_DV), lambda b, h, i: (0, h)),
                  lam_spec, lam_spec, lam_spec, lam_spec],
        out_specs=pl.BlockSpec((tq, A_DV), lambda b, h, i: (b * nq + i, h)),
        out_shape=jax.ShapeDtypeStruct((batch * seq, A_HEADS * A_DV), BF16),
        compiler_params=_cparams("parallel", "parallel", "arbitrary"),
        name="diff_attn_prompt",
    )(aq, akb, avb, slopes, ng, *lams)


def _attn_decode_kernel(pt_ref, q_ref, ks_ref, vs_ref, slope_ref, ng_ref, lq1, lk1, lq2, lk2, *rest,
                        lam_init, pps):
    k_refs, v_refs = rest[:pps], rest[pps:2 * pps]
    o_ref, m_ref, l_ref, acc_ref = rest[2 * pps:]
    j = pl.program_id(1)
    nmap = 2 * A_HEADS
    width = A_HEADS * 2 * A_DK

    @pl.when(j == 0)
    def _():
        m_ref[...] = jnp.full(m_ref.shape, NEG_INF, F32)
        l_ref[...] = jnp.zeros_like(l_ref)
        acc_ref[...] = jnp.zeros_like(acc_ref)

    r8 = lax.broadcasted_iota(I32, (nmap, width), 0)
    c8 = lax.broadcasted_iota(I32, (nmap, width), 1)
    own = (c8 // A_DK) == r8
    qmat = jnp.where(own, jnp.broadcast_to(q_ref[0].astype(F32), (nmap, width)), 0.0)
    qb = qmat.astype(BF16)
    slope = slope_ref[...]
    lane = lax.broadcasted_iota(I32, (nmap, pps * PAGE_SIZE), 1)

    def fold(s):
        m_old = m_ref[...]
        m_new = jnp.maximum(m_old, jnp.max(s, axis=1, keepdims=True))
        alpha = jnp.exp(m_old - m_new)
        p = jnp.exp(s - m_new[:, 0:1])
        m_ref[...] = m_new
        return alpha, p

    q_maps = [jnp.broadcast_to(qb[r:r + 1, r * A_DK:(r + 1) * A_DK], (V7X_SUBLANES, A_DK)) for r in range(nmap)]
    s = jnp.concatenate(
        [jnp.concatenate([_dot(q_maps[r], k_refs[i][0, r // 2, r % 2].astype(BF16))[0:1] for r in range(nmap)],
                         axis=0) for i in range(pps)], axis=1)
    s = s + slope[:, 0:1] * (j * (pps * PAGE_SIZE) + lane).astype(F32)
    alpha, p = fold(s)
    l_ref[...] = alpha * l_ref[...] + jnp.sum(p, axis=1, keepdims=True)
    pv = []
    for h in range(A_HEADS):
        parts = [_dot(p[:, i * PAGE_SIZE:(i + 1) * PAGE_SIZE], v_refs[i][0, pl.ds(h, PAGE_SIZE, stride=A_HEADS), :])
                 for i in range(pps)]
        while len(parts) > 1:
            parts = [a + b for a, b in zip(parts[0::2], parts[1::2])]
        pv.append(parts[0])
    acc_ref[...] = alpha[:, 0:1] * acc_ref[...] + jnp.concatenate(pv, axis=1)

    @pl.when(j == pl.num_programs(1) - 1)
    def _():
        past = jnp.asarray(pl.num_programs(1) * pps * PAGE_SIZE, F32)
        k_new = ks_ref[0].astype(F32)
        v_new = vs_ref[0].astype(F32)
        s = jnp.sum(qmat * k_new, axis=1, keepdims=True) + slope[:, 0:1] * past
        alpha, p = fold(s)
        l = alpha * l_ref[...] + p
        acc = alpha[:, 0:1] * acc_ref[...] + p * v_new
        o = acc / l[:, 0:1]
        lam = _lambda_value(lq1, lk1, lq2, lk2, lam_init)
        d = o - lam * pltpu.roll(o, nmap - 1, axis=0)
        keep = r8 == 2 * (c8 // A_DV)
        hb = jnp.sum(jnp.where(keep, d, 0.0), axis=0, keepdims=True)
        for h in range(A_HEADS):
            hs = slice(h * A_DV, (h + 1) * A_DV)
            o_ref[0, :, hs] = _rms_gain(hb[:, hs], ng_ref[:, hs], lam_init).astype(BF16)


def _attn_decode(page_table, aq, akb, avb, cache_k, cache_v, slopes8, ng, lams, lam_init):
    db, n_pages = page_table.shape
    pps = math.gcd(PAGES_PER_STEP, n_pages)
    width = A_HEADS * 2 * A_DK
    r3 = lambda a: a.reshape(db, 1, a.shape[-1])
    tok = lambda b, j, pt: (b, 0, 0)
    fixed = lambda b, j, pt: (0, 0)
    cache_k = jnp.transpose(cache_k, (0, 2, 3, 4, 1))
    k_specs = [pl.BlockSpec((1,) + cache_k.shape[1:], lambda b, j, pt, i=i: (pt[b, j * pps + i], 0, 0, 0, 0))
               for i in range(pps)]
    cache_v = cache_v.reshape(cache_v.shape[0], PAGE_SIZE * A_HEADS, A_DV)
    v_specs = [pl.BlockSpec((1,) + cache_v.shape[1:], lambda b, j, pt, i=i: (pt[b, j * pps + i], 0, 0))
               for i in range(pps)]
    lam_spec = pl.BlockSpec((1, A_DK), fixed)
    grid_spec = pltpu.PrefetchScalarGridSpec(
        num_scalar_prefetch=1,
        grid=(db, n_pages // pps),
        in_specs=[pl.BlockSpec((1, 1, width), tok), pl.BlockSpec((1, 1, width), tok),
                  pl.BlockSpec((1, 1, width), tok),
                  pl.BlockSpec((2 * A_HEADS, V7X_LANES), fixed), pl.BlockSpec((1, width), fixed),
                  lam_spec, lam_spec, lam_spec, lam_spec] + k_specs + v_specs,
        out_specs=pl.BlockSpec((1, 1, width), tok),
        scratch_shapes=[pltpu.VMEM((2 * A_HEADS, V7X_LANES), F32), pltpu.VMEM((2 * A_HEADS, V7X_LANES), F32),
                        pltpu.VMEM((2 * A_HEADS, width), F32)],
    )
    out = pl.pallas_call(
        functools.partial(_attn_decode_kernel, lam_init=lam_init, pps=pps),
        grid_spec=grid_spec,
        out_shape=jax.ShapeDtypeStruct((db, 1, width), BF16),
        compiler_params=_cparams("parallel", "arbitrary"),
        name="diff_attn_decode",
    )(page_table, r3(aq), r3(akb), r3(avb), slopes8, ng, *lams, *([cache_k] * pps), *([cache_v] * pps))
    return out.reshape(db, width)


def _layer_norm(x, g, b):
    mu = jnp.mean(x, axis=-1, keepdims=True)
    var = jnp.mean(jnp.square(x - mu), axis=-1, keepdims=True)
    return (x - mu) * lax.rsqrt(var + LN_EPS) * g + b


def _merge_kernel(x_ref, ya_ref, yb_ref, sga_ref, sgb_ref, p_ref, wa_ref, wb_ref, wo_ref, g1_ref, b1_ref,
                  wg_ref, wp_ref, wq_ref, x1_ref, base_ref, q_ref, *, alpha):
    ya = _dot(ya_ref[...], wa_ref[...])
    yb = _dot(yb_ref[...], wb_ref[...])
    y = sga_ref[...].astype(F32) * ya + sgb_ref[...].astype(F32) * yb
    y = _dot(y.astype(BF16), wo_ref[...])
    x1 = _layer_norm(alpha * x_ref[...] + y, g1_ref[...], b1_ref[...])
    x1_ref[...] = x1
    x1b = x1.astype(BF16)
    pe = jax.nn.sigmoid(_dot(x1b, wg_ref[...])) * _dot(p_ref[...].astype(BF16), wp_ref[...])
    base_ref[...] = alpha * x1 + pe
    q_ref[...] = _dot(x1b, wq_ref[...]).astype(BF16)


def _merge(x2, ya, yb, sga, sgb, p2, wa, wb, wo, g1, b1, wg, wp, wq, alpha):
    t, d = x2.shape
    tm = min(ROW_TILE, t)
    row = lambda i: (i, 0)
    fixed = lambda i: (0, 0)
    rows = lambda a: pl.BlockSpec((tm, a.shape[1]), row)
    full = lambda a: pl.BlockSpec(a.shape, fixed)
    nq = wq.shape[1]
    return pl.pallas_call(
        functools.partial(_merge_kernel, alpha=alpha),
        grid=(t // tm,),
        in_specs=[rows(x2), rows(ya), rows(yb), rows(sga), rows(sgb), rows(p2),
                  full(wa), full(wb), full(wo), full(g1), full(b1), full(wg), full(wp), full(wq)],
        out_specs=[pl.BlockSpec((tm, d), row), pl.BlockSpec((tm, d), row), pl.BlockSpec((tm, nq), row)],
        out_shape=[jax.ShapeDtypeStruct((t, d), F32), jax.ShapeDtypeStruct((t, d), F32),
                   jax.ShapeDtypeStruct((t, nq), BF16)],
        compiler_params=_cparams("parallel"),
        name="merge_ln1",
    )(x2, ya, yb, sga, sgb, p2, wa, wb, wo, g1, b1, wg, wp, wq)


def _topk_rows(s, k, payload=None):
    n = s.shape[0]
    rows = lax.broadcasted_iota(I32, s.shape, 0).astype(F32)
    vals, picks = [], []
    for _ in range(k):
        mx = jnp.max(s, axis=0, keepdims=True)
        idx = jnp.min(jnp.where(s == mx, rows, float(n)), axis=0, keepdims=True)
        sel = rows == idx
        if payload is None:
            picks.append(idx)
        else:
            picks.append(jnp.sum(jnp.where(sel, payload, 0.0), axis=0, keepdims=True))
        vals.append(mx)
        s = jnp.where(sel, NEG_INF, s)
    return jnp.concatenate(vals, axis=0), jnp.concatenate(picks, axis=0)


def _pair_candidates(sv, si):
    sub = lax.broadcasted_iota(I32, (V7X_SUBLANES,) + sv[0].shape[1:], 0)
    pair = lambda x, y, mul: x * mul + y
    vals = [sv[0][0:1] + sv[1], sv[0][1:2] + sv[1][0:V7X_SUBLANES]]
    idxs = [pair(si[0][0:1], si[1], float(P_NKEYS)), pair(si[0][1:2], si[1][0:V7X_SUBLANES], float(P_NKEYS))]
    for a in range(2, V7X_SUBLANES):
        keep = sub < P_TOPK // (a + 1)
        vals.append(jnp.where(keep, sv[0][a:a + 1] + sv[1][0:V7X_SUBLANES], NEG_INF))
        idxs.append(pair(si[0][a:a + 1], si[1][0:V7X_SUBLANES], float(P_NKEYS)))
    vals.append(sv[0][V7X_SUBLANES:] + sv[1][0:1])
    idxs.append(pair(si[0][V7X_SUBLANES:], si[1][0:1], float(P_NKEYS)))
    return jnp.concatenate(vals, axis=0), jnp.concatenate(idxs, axis=0)


def _route_kernel(q_ref, sk_ref, idx_ref, g_ref, et_ref, gt_ref):
    h = pl.program_id(1)
    rows = pl.ds(pl.multiple_of(h * P_TOPK, P_TOPK), P_TOPK)
    for part in range(q_ref.shape[0] // V7X_LANES):
        ts = slice(part * V7X_LANES, (part + 1) * V7X_LANES)
        sv, si = [], []
        for c in range(2):
            s = _nt_dot(sk_ref[0, c], q_ref[ts, c * P_DHALF:(c + 1) * P_DHALF])
            v, i = _topk_rows(s, P_TOPK)
            sv.append(v)
            si.append(i)
        cand, cidx = _pair_candidates(sv, si)
        cv, eidx = _topk_rows(cand, P_TOPK, payload=cidx)
        e = jnp.exp(cv - cv[0:1])
        et_ref[rows, ts] = eidx
        gt_ref[rows, ts] = e / jnp.sum(e, axis=0, keepdims=True)

    @pl.when(h == P_HEADS - 1)
    def _():
        idx_ref[...] = et_ref[...].T.astype(I32)
        g_ref[...] = gt_ref[...].T


def _route(q, subkeys, tm):
    t = q.shape[0]
    out = pl.BlockSpec((tm, N_SEL), lambda i, h: (i, 0))
    return pl.pallas_call(
        _route_kernel,
        grid=(t // tm, P_HEADS),
        in_specs=[pl.BlockSpec((tm, 2 * P_DHALF), lambda i, h: (i, h)),
                  pl.BlockSpec((1, 2, P_NKEYS, P_DHALF), lambda i, h: (h, 0, 0, 0))],
        out_specs=[out, out],
        out_shape=[jax.ShapeDtypeStruct((t, N_SEL), I32), jax.ShapeDtypeStruct((t, N_SEL), F32)],
        scratch_shapes=[pltpu.VMEM((N_SEL, tm), F32), pltpu.VMEM((N_SEL, tm), F32)],
        compiler_params=_cparams("parallel", "arbitrary"),
        name="peer_route",
    )(q, subkeys)


def _load_row(tab_ref, e):
    return tab_ref[e].astype(F32)


def _sublane_merge(x, y, sh, mask):
    c = jnp.where(mask, x, y)
    d = jnp.where(mask, y, x)
    if 2 * sh == V7X_SUBLANES:
        return c + pltpu.roll(d, sh, axis=0)
    return c + jnp.where(mask, pltpu.roll(d, V7X_SUBLANES - sh, axis=0), pltpu.roll(d, sh, axis=0))


def _stage_blocks(i, nblk, tab_hbm, tab_vmem, per_block, sems):
    slot = i % 2

    def copies(blk, sl):
        return [pltpu.make_async_copy(h.at[blk], s.at[pl.ds(sl * h.shape[1], h.shape[1])],
                                      sems.at[1 + 2 * n + sl])
                for n, (h, s) in enumerate(per_block)]

    @pl.when(i == 0)
    def _():
        table = pltpu.make_async_copy(tab_hbm, tab_vmem, sems.at[0])
        table.start()
        for c in copies(0, 0):
            c.start()
        table.wait()

    for c in copies(i, slot):
        c.wait()

    @pl.when(i + 1 < nblk)
    def _():
        for c in copies(i + 1, 1 - slot):
            c.start()

    return slot


def _peer_up_kernel(x_ref, g_ref, idx_hbm, tab_hbm, w_ref, tab_vmem, idx_smem, part_ref, sems):
    i = pl.program_id(0)
    slot = _stage_blocks(i, pl.num_programs(0), tab_hbm, tab_vmem, [(idx_hbm, idx_smem)], sems)
    tb = x_ref.shape[0]
    chunk = part_ref.shape[0] // N_SEL
    sub = lax.broadcasted_iota(I32, (V7X_SUBLANES, V7X_LANES), 0)
    masks = {sh: (sub & sh) == 0 for sh in (4, 2, 1)}
    ones = jnp.ones((V7X_LANES, V7X_LANES), BF16)
    diag = (lax.broadcasted_iota(I32, (1, N_SEL, V7X_LANES), 1)
            == lax.broadcasted_iota(I32, (1, N_SEL, V7X_LANES), 2))

    def token_chunk(c, carry):
        def token(tt, carry):
            t = c * chunk + tt
            xt = x_ref[t]
            tok_base = (slot * tb + t) * N_SEL

            def group(j, carry):
                col = pl.multiple_of(j * PEER_GROUP, PEER_GROUP)
                for k in range(PEER_GROUP // V7X_SUBLANES):
                    first = col + k * V7X_SUBLANES
                    parts = [_load_row(tab_vmem, idx_smem[tok_base + first + r]) * xt
                             for r in range(V7X_SUBLANES)]
                    for sh in (4, 2, 1):
                        parts = [_sublane_merge(parts[a], parts[a + sh], sh, masks[sh])
                                 for a in range(len(parts)) if (a & sh) == 0]
                    part_ref[pl.ds(pl.multiple_of(tt * N_SEL + first, V7X_SUBLANES), V7X_SUBLANES), :] = parts[0]
                return carry

            return lax.fori_loop(0, N_SEL // PEER_GROUP, group, carry)

        lax.fori_loop(0, chunk, token, 0)
        part = part_ref[...]
        hi = part.astype(BF16)
        lo = (part - hi.astype(F32)).astype(BF16)
        tot = (_dot(hi, ones) + _dot(lo, ones)).reshape(chunk, N_SEL, V7X_LANES)
        s = jnp.sum(jnp.where(diag, tot, 0.0), axis=1)
        act = 0.5 * s * (1.0 + lax.erf(s * (2.0 ** -0.5)))
        rows = pl.ds(pl.multiple_of(c * chunk, chunk), chunk)
        w_ref[rows, :] = g_ref[rows, :] * act
        return carry

    lax.fori_loop(0, tb // chunk, token_chunk, 0)


def _peer_down_kernel(w_ref, idx_hbm, tab_hbm, f_ref, tab_vmem, idx_smem, sems):
    i = pl.program_id(0)
    slot = _stage_blocks(i, pl.num_programs(0), tab_hbm, tab_vmem, [(idx_hbm, idx_smem)], sems)
    tb = f_ref.shape[0]
    n_acc = 4
    eye = lax.broadcasted_iota(I32, (N_SEL, V7X_LANES), 0) == lax.broadcasted_iota(I32, (N_SEL, V7X_LANES), 1)
    ones = jnp.ones((V7X_LANES, V7X_LANES), BF16)

    def splat_rows(t):
        d = jnp.where(eye, jnp.broadcast_to(w_ref[t], (N_SEL, V7X_LANES)), 0.0)
        hi = d.astype(BF16)
        lo = (d - hi.astype(F32)).astype(BF16)
        return _dot(hi, ones) + _dot(lo, ones)

    def token(t, wl):
        wl_next = splat_rows(jnp.minimum(t + 1, tb - 1))
        tok_base = (slot * tb + t) * N_SEL
        accs = [None] * n_acc
        for r in range(N_SEL):
            term = wl[r:r + 1, :] * _load_row(tab_vmem, idx_smem[tok_base + r])
            accs[r % n_acc] = term if accs[r % n_acc] is None else accs[r % n_acc] + term
        f_ref[t] = (accs[0] + accs[1]) + (accs[2] + accs[3])
        return wl_next

    lax.fori_loop(0, tb, token, splat_rows(0))


def _peer_experts(x1, g, idx, u_tab, v_tab, tb):
    t, d = x1.shape
    nblk = t // tb
    x3 = x1.reshape(t, V7X_SUBLANES, d // V7X_SUBLANES)
    idx = idx.reshape(nblk, tb * N_SEL)
    any_spec = pl.BlockSpec(memory_space=pl.ANY)
    w = pl.pallas_call(
        _peer_up_kernel,
        grid=(nblk,),
        in_specs=[pl.BlockSpec((tb,) + x3.shape[1:], lambda i: (i, 0, 0)),
                  pl.BlockSpec((tb, N_SEL), lambda i: (i, 0)), any_spec, any_spec],
        out_specs=pl.BlockSpec((tb, N_SEL), lambda i: (i, 0)),
        out_shape=jax.ShapeDtypeStruct((t, N_SEL), F32),
        scratch_shapes=[pltpu.VMEM(u_tab.shape, u_tab.dtype), pltpu.SMEM((2 * tb * N_SEL,), I32),
                        pltpu.VMEM((PEER_CHUNK * N_SEL, V7X_LANES), F32), pltpu.SemaphoreType.DMA((3,))],
        compiler_params=_cparams("arbitrary"),
        name="peer_up",
    )(x3, g, idx, u_tab)
    f3 = pl.pallas_call(
        _peer_down_kernel,
        grid=(nblk,),
        in_specs=[pl.BlockSpec((tb, 1, N_SEL), lambda i: (i, 0, 0)), any_spec, any_spec],
        out_specs=pl.BlockSpec((tb,) + x3.shape[1:], lambda i: (i, 0, 0)),
        out_shape=jax.ShapeDtypeStruct(x3.shape, F32),
        scratch_shapes=[pltpu.VMEM(v_tab.shape, v_tab.dtype), pltpu.SMEM((2 * tb * N_SEL,), I32),
                        pltpu.SemaphoreType.DMA((3,))],
        compiler_params=_cparams("arbitrary"),
        name="peer_down",
    )(w.reshape(t, 1, N_SEL), idx, v_tab)
    return f3.reshape(t, d)


def _final_kernel(base_ref, f_ref, g_ref, b_ref, o_ref):
    o_ref[...] = _layer_norm(base_ref[...] + f_ref[...], g_ref[...], b_ref[...])


def _final(base, f, g2, b2):
    t, d = base.shape
    tm = min(ROW_TILE, t)
    row = pl.BlockSpec((tm, d), lambda i: (i, 0))
    vec = pl.BlockSpec((1, d), lambda i: (0, 0))
    return pl.pallas_call(
        _final_kernel,
        grid=(t // tm,),
        in_specs=[row, row, vec, vec],
        out_specs=row,
        out_shape=jax.ShapeDtypeStruct((t, d), F32),
        compiler_params=_cparams("parallel"),
        name="residual_ln2",
    )(base, f, g2, b2)


def _reorder_in_proj(w, b):
    hk = M_HEADS * M_DK
    widths = (hk, hk, M_HEADS * M_DV, M_HEADS * M_DV, M_HEADS, M_HEADS,
              A_HEADS * 2 * A_DK, A_HEADS * 2 * A_DK, A_HEADS * A_DV)
    cuts = [0]
    for n in widths:
        cuts.append(cuts[-1] + n)
    total = w.shape[1]
    d_model = (total - cuts[-1]) // 2
    cuts += [cuts[-1] + d_model, total]
    piece = lambda a, i: a[..., cuts[i]:cuts[i + 1]]
    order = (0, 1, 2, 3, 6, 7, 8, 9, 10, 4, 5)
    pad = V7X_LANES - 2 * M_HEADS
    wr = jnp.concatenate([piece(w, i) for i in order] + [jnp.zeros((w.shape[0], pad), w.dtype)], axis=1)
    br = jnp.concatenate([piece(b, i) for i in order] + [jnp.zeros((pad,), b.dtype)], axis=0)
    return wr.astype(BF16), br.reshape(1, -1)


def _table_tiles(tab):
    n, d = tab.shape
    return tab.astype(BF16).reshape(n, V7X_SUBLANES, d // V7X_SUBLANES)


def _token_local(x2, ya, yb, sga, sgb, p2, lw, alpha):
    t = x2.shape[0]
    if t % PEER_TILE:
        pad = lambda a: jnp.pad(a, ((0, PEER_TILE - t % PEER_TILE), (0, 0)))
        return _token_local(pad(x2), pad(ya), pad(yb), pad(sga), pad(sgb), pad(p2), lw, alpha)[:t]
    x1, base, q = _merge(x2, ya, yb, sga, sgb, p2, lw["wa"], lw["wb"], lw["wo"], lw["g1"], lw["b1"],
                         lw["wg"], lw["wp"], lw["wq"], alpha)
    idx, g = _route(q, lw["subkeys"], ROUTE_TILE if t % ROUTE_TILE == 0 else PEER_TILE)
    f = _peer_experts(x1, g, idx, lw["u_tab"], lw["v_tab"], PEER_TILE)
    return _final(base, f, lw["g2"], lw["b2"])


def kernel(x_prompt, x_sample, cache_k, cache_v, state_C, state_n, state_m, page_table, p_prompt, p_sample,
           w_in, b_in, lambda_q1, lambda_k1, lambda_q2, lambda_k2, mlstm_norm_g, diff_norm_g, w_branch_a,
           w_branch_b, w_out, ln1_g, ln1_b, peer_wq, peer_subkeys, peer_u, peer_v, ple_w_gate, ple_w_proj,
           ln2_g, ln2_b):
    depth = w_in.shape[0]
    batch, seq, d_model = x_prompt.shape
    db, dseq, _ = x_sample.shape
    assert dseq == 1, "the decode path handles one new token per request"
    alpha = (2.0 * depth) ** 0.25
    slopes = 2.0 ** (-8.0 * (jnp.arange(A_HEADS, dtype=F32) + 1.0) / A_HEADS)
    slopes_h = jnp.broadcast_to(slopes[:, None, None], (A_HEADS, 1, V7X_LANES))
    slopes_8 = jnp.broadcast_to(jnp.repeat(slopes, 2)[:, None], (2 * A_HEADS, V7X_LANES))

    hp = x_prompt.reshape(batch * seq, d_model)
    hs = x_sample.reshape(db, d_model)
    outs = [[] for _ in range(10)]
    for l in range(depth):
        lam_init = 0.8 - 0.6 * math.exp(-0.3 * l)
        w_l, b_l = _reorder_in_proj(w_in[l], b_in[l])
        row = lambda a: a.reshape(1, -1)
        lw = dict(wa=w_branch_a[l].astype(BF16), wb=w_branch_b[l].astype(BF16), wo=w_out[l].astype(BF16),
                  g1=row(ln1_g[l]), b1=row(ln1_b[l]), wg=ple_w_gate[l].astype(BF16),
                  wp=ple_w_proj[l].astype(BF16), wq=peer_wq[l].astype(BF16),
                  subkeys=peer_subkeys[l].astype(BF16), u_tab=_table_tiles(peer_u[l]),
                  v_tab=_table_tiles(peer_v[l]), g2=row(ln2_g[l]), b2=row(ln2_b[l]))
        lams = [row(a[l].astype(F32)) for a in (lambda_q1, lambda_k1, lambda_q2, lambda_k2)]
        mng = row(mlstm_norm_g[l])
        dng = row(diff_norm_g[l])

        mq, mk, mv, so, aq, akt, avf, akb, avb, sga, sgb, gates = _project(hp, w_l, b_l, seq=seq)
        ya, c_p, n_p, m_p = _mlstm_prompt(mq, mk, mv, so, gates, mng, batch, seq)
        yb = _attn_prompt(aq, akb, avb, slopes_h, dng, lams, batch, seq, lam_init)
        hp = _token_local(hp, ya, yb, sga, sgb, p_prompt[l].reshape(batch * seq, -1), lw, alpha)
        outs[0].append(jnp.transpose(akt.reshape(batch, A_HEADS, 2, A_DK, seq), (0, 4, 1, 2, 3)))
        outs[1].append(avf.reshape(batch, seq, A_HEADS, A_DV))
        outs[2].append(c_p)
        outs[3].append(n_p)
        outs[4].append(m_p[:, 0, :M_HEADS])

        mq, mk, mv, so, aq, akf, avf, akb, avb, sga, sgb, gates = _project(hs, w_l, b_l)
        m0 = jnp.pad(state_m[l].astype(F32), ((0, 0), (0, V7X_LANES - M_HEADS))).reshape(db, 1, V7X_LANES)
        ya, c_s, n_s, m_s = _mlstm_step(mq, mk, mv, so, gates, mng, state_C[l].astype(F32),
                                        state_n[l].astype(F32), m0)
        yb = _attn_decode(page_table, aq, akb, avb, cache_k[l], cache_v[l], slopes_8, dng, lams, lam_init)
        hs = _token_local(hs, ya.reshape(db, -1), yb, sga, sgb, p_sample[l].reshape(db, -1), lw, alpha)
        outs[5].append(akf.reshape(db, 1, A_HEADS, 2, A_DK))
        outs[6].append(avf.reshape(db, 1, A_HEADS, A_DV))
        outs[7].append(c_s)
        outs[8].append(n_s)
        outs[9].append(m_s[:, 0, :M_HEADS])

    st = [jnp.stack(o, 0) for o in outs]
    return (hp.reshape(batch, seq, d_model), hs.reshape(db, 1, d_model),
            st[0], st[1], st[2], st[3], st[4], st[5], st[6], st[7], st[8], st[9])
```

```python
import functools
import math

import jax
import jax.numpy as jnp
from jax import lax
from jax.experimental import pallas as pl
from jax.experimental.pallas import tpu as pltpu

F32 = jnp.float32
BF16 = jnp.bfloat16
I32 = jnp.int32

M_HEADS, M_DK, M_DV = 4, 128, 128
A_HEADS, A_DK, A_DV = 4, 64, 128
P_HEADS, P_NKEYS, P_DHALF, P_TOPK = 8, 128, 128, 16
PAGE_SIZE = 128
LN_EPS = 1e-5
N_SEL = P_HEADS * P_TOPK

V7X_LANES = 128
V7X_SUBLANES = 8
V7X_VMEM_LIMIT = 56 * 1024 * 1024

MLSTM_CHUNK = 128
ATTN_BLOCK = 256
PAGES_PER_STEP = 16
ROW_TILE = 256
PEER_TILE = 128
PEER_GROUP = 128
PEER_CHUNK = 32
ROUTE_TILE = 256

NEG_INF = float("-inf")


def _cparams(*sem):
    return pltpu.CompilerParams(dimension_semantics=sem, vmem_limit_bytes=V7X_VMEM_LIMIT)


def _nt_dot(a, b):
    return lax.dot_general(a, b, (((1,), (1,)), ((), ())), preferred_element_type=F32)


def _dot(a, b):
    return jnp.dot(a, b, preferred_element_type=F32)


_C_MQ, _C_MK, _C_MV, _C_MO, _C_AQ, _C_AK, _C_AV, _C_GA = 0, 512, 1024, 1536, 2048, 2560, 3072, 3584
_C_GB, _C_GATE, _C_END = 4608, 5632, 5760


def _proj_kernel(x_ref, w_ref, b_ref, mq, mk, mv, so, aq, akf, avf, akb, avb, sga, sgb, gates, *, k_by_position):
    xb = x_ref[...].astype(BF16)

    def seg(lo, hi):
        return _dot(xb, w_ref[:, lo:hi]) + b_ref[:, lo:hi]

    mq[...] = seg(_C_MQ, _C_MK).astype(BF16)
    mk[...] = (seg(_C_MK, _C_MV) * (M_DK ** -0.5)).astype(BF16)
    mv[...] = seg(_C_MV, _C_MO).astype(BF16)
    so[...] = jax.nn.sigmoid(seg(_C_MO, _C_AQ)).astype(BF16)
    aq[...] = (seg(_C_AQ, _C_AK) * (A_DK ** -0.5)).astype(BF16)
    k = seg(_C_AK, _C_AV)
    if k_by_position:
        akf[0] = k.T
    else:
        akf[...] = k
    akb[...] = k.astype(BF16)
    v = seg(_C_AV, _C_GA)
    avf[...] = v
    avb[...] = v.astype(BF16)
    sga[...] = jax.nn.sigmoid(seg(_C_GA, _C_GB)).astype(BF16)
    sgb[...] = jax.nn.sigmoid(seg(_C_GB, _C_GATE)).astype(BF16)
    gates[...] = seg(_C_GATE, _C_END)


def _project(x2, w, b, seq=None):
    t, d = x2.shape
    tm = min(ROW_TILE, t)
    row = lambda i: (i, 0)
    fixed = lambda i: (0, 0)
    widths = [(512, BF16)] * 5 + [(512, F32)] * 2 + [(512, BF16)] * 2 + [(1024, BF16)] * 2 + [(128, F32)]
    out_specs = [pl.BlockSpec((tm, n), row) for n, _ in widths]
    out_shape = [jax.ShapeDtypeStruct((t, n), dt) for n, dt in widths]
    if seq is not None:
        per_seq = seq // tm
        out_specs[5] = pl.BlockSpec((1, widths[5][0], tm), lambda i: (i // per_seq, 0, i % per_seq))
        out_shape[5] = jax.ShapeDtypeStruct((t // seq, widths[5][0], seq), F32)
    return pl.pallas_call(
        functools.partial(_proj_kernel, k_by_position=seq is not None),
        grid=(t // tm,),
        in_specs=[pl.BlockSpec((tm, d), row), pl.BlockSpec(w.shape, fixed), pl.BlockSpec(b.shape, fixed)],
        out_specs=out_specs,
        out_shape=out_shape,
        compiler_params=_cparams("parallel"),
        name="project_in",
    )(x2, w, b)


def _head_ln_gate(hh, g_row, so):
    mu = jnp.mean(hh, axis=-1, keepdims=True)
    var = jnp.mean(jnp.square(hh - mu), axis=-1, keepdims=True)
    return ((hh - mu) * lax.rsqrt(var + LN_EPS) * g_row * so.astype(F32)).astype(BF16)


def _mlstm_prompt_kernel(q_ref, k_ref, v_ref, so_ref, g_ref, ng_ref, ya_ref, c_ref, n_ref, m_ref):
    @pl.when(pl.program_id(1) == 0)
    def _():
        c_ref[...] = jnp.zeros_like(c_ref)
        n_ref[...] = jnp.zeros_like(n_ref)
        m_ref[...] = jnp.zeros_like(m_ref)

    L = q_ref.shape[0]
    gate = g_ref[...]
    cum = jax.nn.log_sigmoid(gate)
    row = lax.broadcasted_iota(I32, cum.shape, 0)
    sh = 1
    while sh < L:
        cum = cum + jnp.where(row >= sh, pltpu.roll(cum, sh, axis=0), 0.0)
        sh *= 2
    cum_t = cum.T
    gate_t = gate.T
    causal = lax.broadcasted_iota(I32, (L, L), 1) <= lax.broadcasted_iota(I32, (L, L), 0)
    lane = lax.broadcasted_iota(I32, (1, V7X_LANES), 1)
    m_all = m_ref[0]
    m_next = m_all
    for h in range(M_HEADS):
        hs = slice(h * M_DK, (h + 1) * M_DK)
        b_col = cum[:, M_HEADS + h:M_HEADS + h + 1]
        b_row = cum_t[M_HEADS + h:M_HEADS + h + 1, :]
        i_col = gate[:, h:h + 1]
        i_row = gate_t[h:h + 1, :]
        m0 = m_all[:, h:h + 1]
        dmat = jnp.where(causal, b_col - b_row + i_row, NEG_INF)
        inter = b_col + m0
        m = jnp.maximum(inter, jnp.max(dmat, axis=1, keepdims=True))
        w = jnp.exp(dmat - m)
        a = jnp.exp(inter - m)
        qh, kh, vh = q_ref[:, hs], k_ref[:, hs], v_ref[:, hs]
        c0 = c_ref[0, h]
        n0 = n_ref[0, h:h + 1, :]
        qk = _nt_dot(qh, kh) * w
        num = _dot(qk.astype(BF16), vh) + a * _dot(qh, c0.astype(BF16))
        den = jnp.sum(qk, axis=1, keepdims=True) + a * jnp.sum(qh.astype(F32) * n0, axis=1, keepdims=True)
        hh = num / jnp.maximum(jnp.abs(den), jnp.exp(-m))
        ya_ref[:, hs] = _head_ln_gate(hh, ng_ref[:, hs], so_ref[:, hs])
        m_last = m[L - 1:L, :]
        b_last = b_col[L - 1:L, :]
        w_last = jnp.exp(b_last - b_col + i_col - m_last)
        a_last = jnp.exp(b_last + m0 - m_last)
        kw = kh.astype(F32) * w_last
        c_ref[0, h] = a_last * c0 + _dot(kw.T.astype(BF16), vh)
        n_ref[0, h:h + 1, :] = a_last * n0 + jnp.sum(kw, axis=0, keepdims=True)
        m_next = jnp.where(lane == h, m_last, m_next)
    m_ref[0] = m_next


def _mlstm_prompt(mq, mk, mv, so, gates, ng, batch, seq):
    L = min(MLSTM_CHUNK, seq)
    nc = seq // L
    tok = lambda b, c: (b * nc + c, 0)
    wide = pl.BlockSpec((L, M_HEADS * M_DK), tok)
    return pl.pallas_call(
        _mlstm_prompt_kernel,
        grid=(batch, nc),
        in_specs=[wide, wide, wide, wide, pl.BlockSpec((L, V7X_LANES), tok),
                  pl.BlockSpec((1, M_HEADS * M_DV), lambda b, c: (0, 0))],
        out_specs=[wide,
                   pl.BlockSpec((1, M_HEADS, M_DK, M_DV), lambda b, c: (b, 0, 0, 0)),
                   pl.BlockSpec((1, M_HEADS, M_DK), lambda b, c: (b, 0, 0)),
                   pl.BlockSpec((1, 1, V7X_LANES), lambda b, c: (b, 0, 0))],
        out_shape=[jax.ShapeDtypeStruct((batch * seq, M_HEADS * M_DV), BF16),
                   jax.ShapeDtypeStruct((batch, M_HEADS, M_DK, M_DV), F32),
                   jax.ShapeDtypeStruct((batch, M_HEADS, M_DK), F32),
                   jax.ShapeDtypeStruct((batch, 1, V7X_LANES), F32)],
        compiler_params=_cparams("parallel", "arbitrary"),
        name="mlstm_prompt",
    )(mq, mk, mv, so, gates, ng)


def _mlstm_step_kernel(q_ref, k_ref, v_ref, so_ref, g_ref, ng_ref, c0_ref, n0_ref, m0_ref,
                       ya_ref, c_ref, n_ref, m_ref):
    gate = g_ref[0]
    m_all = m0_ref[0]
    lane = lax.broadcasted_iota(I32, (1, V7X_LANES), 1)
    diag = lax.broadcasted_iota(I32, (M_DK, M_DK), 0) == lax.broadcasted_iota(I32, (M_DK, M_DK), 1)
    m_next = m_all
    for h in range(M_HEADS):
        hs = slice(h * M_DK, (h + 1) * M_DK)
        q = q_ref[0][:, hs].astype(F32)
        k = k_ref[0][:, hs].astype(F32)
        v = v_ref[0][:, hs].astype(F32)
        ig = gate[:, h:h + 1]
        lf = jax.nn.log_sigmoid(gate[:, M_HEADS + h:M_HEADS + h + 1])
        m0 = m_all[:, h:h + 1]
        inter = lf + m0
        m = jnp.maximum(inter, ig)
        w = jnp.exp(ig - m)
        a = jnp.exp(inter - m)
        c0 = c0_ref[0, h]
        n0 = n0_ref[0, h:h + 1, :]
        qk = jnp.sum(q * k, axis=1, keepdims=True) * w
        q8 = jnp.broadcast_to(q, (V7X_SUBLANES, M_DK)).astype(BF16)
        qc = _dot(q8, c0.astype(BF16))[0:1, :]
        num = qk * v + a * qc
        den = qk + a * jnp.sum(q * n0, axis=1, keepdims=True)
        hh = num / jnp.maximum(jnp.abs(den), jnp.exp(-m))
        ya_ref[0, :, hs] = _head_ln_gate(hh, ng_ref[:, hs], so_ref[0][:, hs])
        kd = jnp.where(diag, jnp.broadcast_to(k, (M_DK, M_DK)), 0.0).astype(BF16)
        vb = jnp.broadcast_to(v, (M_DK, M_DV)).astype(BF16)
        c_ref[0, h] = a * c0 + w * _dot(kd, vb)
        n_ref[0, h:h + 1, :] = a * n0 + w * k
        m_next = jnp.where(lane == h, m, m_next)
    m_ref[0] = m_next


def _mlstm_step(mq, mk, mv, so, gates, ng, c0, n0, m0):
    db = mq.shape[0]
    r3 = lambda a: a.reshape(db, 1, a.shape[-1])
    tok = lambda b: (b, 0, 0)
    wide = pl.BlockSpec((1, 1, M_HEADS * M_DK), tok)
    narrow = pl.BlockSpec((1, 1, V7X_LANES), tok)
    cspec = pl.BlockSpec((1, M_HEADS, M_DK, M_DV), lambda b: (b, 0, 0, 0))
    nspec = pl.BlockSpec((1, M_HEADS, M_DK), tok)
    return pl.pallas_call(
        _mlstm_step_kernel,
        grid=(db,),
        in_specs=[wide, wide, wide, wide, narrow, pl.BlockSpec((1, M_HEADS * M_DV), lambda b: (0, 0)),
                  cspec, nspec, narrow],
        out_specs=[wide, cspec, nspec, narrow],
        out_shape=[jax.ShapeDtypeStruct((db, 1, M_HEADS * M_DV), BF16),
                   jax.ShapeDtypeStruct(c0.shape, F32),
                   jax.ShapeDtypeStruct(n0.shape, F32),
                   jax.ShapeDtypeStruct((db, 1, V7X_LANES), F32)],
        compiler_params=_cparams("parallel"),
        name="mlstm_step",
    )(r3(mq), r3(mk), r3(mv), r3(so), r3(gates), ng, c0, n0, m0)


def _lambda_value(lq1, lk1, lq2, lk2, lam_init):
    s1 = jnp.sum(lq1[...] * lk1[...], axis=1, keepdims=True)
    s2 = jnp.sum(lq2[...] * lk2[...], axis=1, keepdims=True)
    return jnp.exp(s1) - jnp.exp(s2) + lam_init


def _rms_gain(o, g_row, lam_init):
    return o * lax.rsqrt(jnp.mean(jnp.square(o), axis=-1, keepdims=True) + LN_EPS) * g_row * (1.0 - lam_init)


def _attn_prompt_kernel(q_ref, k_ref, v_ref, slope_ref, ng_ref, lq1, lk1, lq2, lk2, o_ref, *, lam_init):
    tq = q_ref.shape[0]
    qi = pl.program_id(2)
    q = q_ref[...]
    lane = lax.broadcasted_iota(I32, q.shape, 1)
    zero = jnp.zeros_like(q)
    qq = jnp.concatenate([jnp.where(lane < A_DK, q, zero), jnp.where(lane >= A_DK, q, zero)], axis=0)
    slope = slope_ref[0]
    slope_row = jnp.concatenate([slope] * (tq // V7X_LANES), axis=1)
    col = lax.broadcasted_iota(I32, (1, tq), 1)

    def block(j, carry, masked):
        m, l, acc = carry
        start = pl.multiple_of(j * tq, tq)
        kb = k_ref[pl.ds(start, tq), :]
        vb = v_ref[pl.ds(start, tq), :]
        s = _nt_dot(qq, kb) + slope_row * (j * tq + col).astype(F32)
        if masked:
            r = lax.broadcasted_iota(I32, s.shape, 0)
            r = jnp.where(r >= tq, r - tq, r)
            s = jnp.where(lax.broadcasted_iota(I32, s.shape, 1) <= r, s, NEG_INF)
        m_new = jnp.maximum(m, jnp.max(s, axis=1, keepdims=True))
        p = jnp.exp(s - m_new)
        alpha = jnp.exp(m - m_new)
        l = alpha * l + jnp.sum(p, axis=1, keepdims=True)
        acc = alpha * acc + _dot(p.astype(BF16), vb)
        return m_new, l, acc

    init = (jnp.full((2 * tq, 1), NEG_INF, F32), jnp.zeros((2 * tq, 1), F32), jnp.zeros((2 * tq, A_DV), F32))
    carry = lax.fori_loop(0, qi, lambda j, c: block(j, c, False), init)
    _, l, acc = block(qi, carry, True)
    lam = _lambda_value(lq1, lk1, lq2, lk2, lam_init)
    o = acc[:tq] / l[:tq] - lam * (acc[tq:] / l[tq:])
    o_ref[...] = _rms_gain(o, ng_ref[...], lam_init).astype(BF16)


def _attn_prompt(aq, akb, avb, slopes, ng, lams, batch, seq, lam_init):
    tq = min(ATTN_BLOCK, seq)
    nq = seq // tq
    hw = 2 * A_DK
    lam_spec = pl.BlockSpec((1, A_DK), lambda b, h, i: (0, 0))
    return pl.pallas_call(
        functools.partial(_attn_prompt_kernel, lam_init=lam_init),
        grid=(batch, A_HEADS, nq),
        in_specs=[pl.BlockSpec((tq, hw), lambda b, h, i: (b * nq + i, h)),
                  pl.BlockSpec((seq, hw), lambda b, h, i: (b, h)),
                  pl.BlockSpec((seq, A_DV), lambda b, h, i: (b, h)),
                  pl.BlockSpec((1, 1, V7X_LANES), lambda b, h, i: (h, 0, 0)),
                  pl.BlockSpec((1, A_DV), lambda b, h, i: (0, h)),
                  lam_spec, lam_spec, lam_spec, lam_spec],
        out_specs=pl.BlockSpec((tq, A_DV), lambda b, h, i: (b * nq + i, h)),
        out_shape=jax.ShapeDtypeStruct((batch * seq, A_HEADS * A_DV), BF16),
        compiler_params=_cparams("parallel", "parallel", "arbitrary"),
        name="diff_attn_prompt",
    )(aq, akb, avb, slopes, ng, *lams)


def _attn_decode_kernel(pt_ref, q_ref, ks_ref, vs_ref, slope_ref, ng_ref, lq1, lk1, lq2, lk2, *rest,
                        lam_init, pps):
    k_refs, v_refs = rest[:pps], rest[pps:2 * pps]
    o_ref, m_ref, l_ref, acc_ref = rest[2 * pps:]
    j = pl.program_id(1)
    nmap = 2 * A_HEADS
    width = A_HEADS * 2 * A_DK

    @pl.when(j == 0)
    def _():
        m_ref[...] = jnp.full(m_ref.shape, NEG_INF, F32)
        l_ref[...] = jnp.zeros_like(l_ref)
        acc_ref[...] = jnp.zeros_like(acc_ref)

    r8 = lax.broadcasted_iota(I32, (nmap, width), 0)
    c8 = lax.broadcasted_iota(I32, (nmap, width), 1)
    own = (c8 // A_DK) == r8
    qmat = jnp.where(own, jnp.broadcast_to(q_ref[0].astype(F32), (nmap, width)), 0.0)
    qb = qmat.astype(BF16)
    slope = slope_ref[...]
    lane = lax.broadcasted_iota(I32, (nmap, pps * PAGE_SIZE), 1)

    def fold(s):
        m_old = m_ref[...]
        m_new = jnp.maximum(m_old, jnp.max(s, axis=1, keepdims=True))
        alpha = jnp.exp(m_old - m_new)
        p = jnp.exp(s - m_new[:, 0:1])
        m_ref[...] = m_new
        return alpha, p

    q_maps = [jnp.broadcast_to(qb[r:r + 1, r * A_DK:(r + 1) * A_DK], (V7X_SUBLANES, A_DK)) for r in range(nmap)]
    s = jnp.concatenate(
        [jnp.concatenate([_dot(q_maps[r], k_refs[i][0, r // 2, r % 2].astype(BF16))[0:1] for r in range(nmap)],
                         axis=0) for i in range(pps)], axis=1)
    s = s + slope[:, 0:1] * (j * (pps * PAGE_SIZE) + lane).astype(F32)
    alpha, p = fold(s)
    l_ref[...] = alpha * l_ref[...] + jnp.sum(p, axis=1, keepdims=True)
    pv = []
    for h in range(A_HEADS):
        parts = [_dot(p[:, i * PAGE_SIZE:(i + 1) * PAGE_SIZE], v_refs[i][0, pl.ds(h, PAGE_SIZE, stride=A_HEADS), :])
                 for i in range(pps)]
        while len(parts) > 1:
            parts = [a + b for a, b in zip(parts[0::2], parts[1::2])]
        pv.append(parts[0])
    acc_ref[...] = alpha[:, 0:1] * acc_ref[...] + jnp.concatenate(pv, axis=1)

    @pl.when(j == pl.num_programs(1) - 1)
    def _():
        past = jnp.asarray(pl.num_programs(1) * pps * PAGE_SIZE, F32)
        k_new = ks_ref[0].astype(F32)
        v_new = vs_ref[0].astype(F32)
        s = jnp.sum(qmat * k_new, axis=1, keepdims=True) + slope[:, 0:1] * past
        alpha, p = fold(s)
        l = alpha * l_ref[...] + p
        acc = alpha[:, 0:1] * acc_ref[...] + p * v_new
        o = acc / l[:, 0:1]
        lam = _lambda_value(lq1, lk1, lq2, lk2, lam_init)
        d = o - lam * pltpu.roll(o, nmap - 1, axis=0)
        keep = r8 == 2 * (c8 // A_DV)
        hb = jnp.sum(jnp.where(keep, d, 0.0), axis=0, keepdims=True)
        for h in range(A_HEADS):
            hs = slice(h * A_DV, (h + 1) * A_DV)
            o_ref[0, :, hs] = _rms_gain(hb[:, hs], ng_ref[:, hs], lam_init).astype(BF16)


def _attn_decode(page_table, aq, akb, avb, cache_k, cache_v, slopes8, ng, lams, lam_init):
    db, n_pages = page_table.shape
    pps = math.gcd(PAGES_PER_STEP, n_pages)
    width = A_HEADS * 2 * A_DK
    r3 = lambda a: a.reshape(db, 1, a.shape[-1])
    tok = lambda b, j, pt: (b, 0, 0)
    fixed = lambda b, j, pt: (0, 0)
    cache_k = jnp.transpose(cache_k, (0, 2, 3, 4, 1))
    k_specs = [pl.BlockSpec((1,) + cache_k.shape[1:], lambda b, j, pt, i=i: (pt[b, j * pps + i], 0, 0, 0, 0))
               for i in range(pps)]
    cache_v = cache_v.reshape(cache_v.shape[0], PAGE_SIZE * A_HEADS, A_DV)
    v_specs = [pl.BlockSpec((1,) + cache_v.shape[1:], lambda b, j, pt, i=i: (pt[b, j * pps + i], 0, 0))
               for i in range(pps)]
    lam_spec = pl.BlockSpec((1, A_DK), fixed)
    grid_spec = pltpu.PrefetchScalarGridSpec(
        num_scalar_prefetch=1,
        grid=(db, n_pages // pps),
        in_specs=[pl.BlockSpec((1, 1, width), tok), pl.BlockSpec((1, 1, width), tok),
                  pl.BlockSpec((1, 1, width), tok),
                  pl.BlockSpec((2 * A_HEADS, V7X_LANES), fixed), pl.BlockSpec((1, width), fixed),
                  lam_spec, lam_spec, lam_spec, lam_spec] + k_specs + v_specs,
        out_specs=pl.BlockSpec((1, 1, width), tok),
        scratch_shapes=[pltpu.VMEM((2 * A_HEADS, V7X_LANES), F32), pltpu.VMEM((2 * A_HEADS, V7X_LANES), F32),
                        pltpu.VMEM((2 * A_HEADS, width), F32)],
    )
    out = pl.pallas_call(
        functools.partial(_attn_decode_kernel, lam_init=lam_init, pps=pps),
        grid_spec=grid_spec,
        out_shape=jax.ShapeDtypeStruct((db, 1, width), BF16),
        compiler_params=_cparams("parallel", "arbitrary"),
        name="diff_attn_decode",
    )(page_table, r3(aq), r3(akb), r3(avb), slopes8, ng, *lams, *([cache_k] * pps), *([cache_v] * pps))
    return out.reshape(db, width)


def _layer_norm(x, g, b):
    mu = jnp.mean(x, axis=-1, keepdims=True)
    var = jnp.mean(jnp.square(x - mu), axis=-1, keepdims=True)
    return (x - mu) * lax.rsqrt(var + LN_EPS) * g + b


def _merge_kernel(x_ref, ya_ref, yb_ref, sga_ref, sgb_ref, p_ref, wa_ref, wb_ref, wo_ref, g1_ref, b1_ref,
                  wg_ref, wp_ref, wq_ref, x1_ref, base_ref, q_ref, *, alpha):
    ya = _dot(ya_ref[...], wa_ref[...])
    yb = _dot(yb_ref[...], wb_ref[...])
    y = sga_ref[...].astype(F32) * ya + sgb_ref[...].astype(F32) * yb
    y = _dot(y.astype(BF16), wo_ref[...])
    x1 = _layer_norm(alpha * x_ref[...] + y, g1_ref[...], b1_ref[...])
    x1_ref[...] = x1
    x1b = x1.astype(BF16)
    pe = jax.nn.sigmoid(_dot(x1b, wg_ref[...])) * _dot(p_ref[...].astype(BF16), wp_ref[...])
    base_ref[...] = alpha * x1 + pe
    q_ref[...] = _dot(x1b, wq_ref[...]).astype(BF16)


def _merge(x2, ya, yb, sga, sgb, p2, wa, wb, wo, g1, b1, wg, wp, wq, alpha):
    t, d = x2.shape
    tm = min(ROW_TILE, t)
    row = lambda i: (i, 0)
    fixed = lambda i: (0, 0)
    rows = lambda a: pl.BlockSpec((tm, a.shape[1]), row)
    full = lambda a: pl.BlockSpec(a.shape, fixed)
    nq = wq.shape[1]
    return pl.pallas_call(
        functools.partial(_merge_kernel, alpha=alpha),
        grid=(t // tm,),
        in_specs=[rows(x2), rows(ya), rows(yb), rows(sga), rows(sgb), rows(p2),
                  full(wa), full(wb), full(wo), full(g1), full(b1), full(wg), full(wp), full(wq)],
        out_specs=[pl.BlockSpec((tm, d), row), pl.BlockSpec((tm, d), row), pl.BlockSpec((tm, nq), row)],
        out_shape=[jax.ShapeDtypeStruct((t, d), F32), jax.ShapeDtypeStruct((t, d), F32),
                   jax.ShapeDtypeStruct((t, nq), BF16)],
        compiler_params=_cparams("parallel"),
        name="merge_ln1",
    )(x2, ya, yb, sga, sgb, p2, wa, wb, wo, g1, b1, wg, wp, wq)


def _topk_rows(s, k, payload=None):
    n = s.shape[0]
    rows = lax.broadcasted_iota(I32, s.shape, 0).astype(F32)
    vals, picks = [], []
    for _ in range(k):
        mx = jnp.max(s, axis=0, keepdims=True)
        idx = jnp.min(jnp.where(s == mx, rows, float(n)), axis=0, keepdims=True)
        sel = rows == idx
        if payload is None:
            picks.append(idx)
        else:
            picks.append(jnp.sum(jnp.where(sel, payload, 0.0), axis=0, keepdims=True))
        vals.append(mx)
        s = jnp.where(sel, NEG_INF, s)
    return jnp.concatenate(vals, axis=0), jnp.concatenate(picks, axis=0)


def _pair_candidates(sv, si):
    sub = lax.broadcasted_iota(I32, (V7X_SUBLANES,) + sv[0].shape[1:], 0)
    pair = lambda x, y, mul: x * mul + y
    vals = [sv[0][0:1] + sv[1], sv[0][1:2] + sv[1][0:V7X_SUBLANES]]
    idxs = [pair(si[0][0:1], si[1], float(P_NKEYS)), pair(si[0][1:2], si[1][0:V7X_SUBLANES], float(P_NKEYS))]
    for a in range(2, V7X_SUBLANES):
        keep = sub < P_TOPK // (a + 1)
        vals.append(jnp.where(keep, sv[0][a:a + 1] + sv[1][0:V7X_SUBLANES], NEG_INF))
        idxs.append(pair(si[0][a:a + 1], si[1][0:V7X_SUBLANES], float(P_NKEYS)))
    vals.append(sv[0][V7X_SUBLANES:] + sv[1][0:1])
    idxs.append(pair(si[0][V7X_SUBLANES:], si[1][0:1], float(P_NKEYS)))
    return jnp.concatenate(vals, axis=0), jnp.concatenate(idxs, axis=0)


def _route_kernel(q_ref, sk_ref, idx_ref, g_ref, et_ref, gt_ref):
    h = pl.program_id(1)
    rows = pl.ds(pl.multiple_of(h * P_TOPK, P_TOPK), P_TOPK)
    for part in range(q_ref.shape[0] // V7X_LANES):
        ts = slice(part * V7X_LANES, (part + 1) * V7X_LANES)
        sv, si = [], []
        for c in range(2):
            s = _nt_dot(sk_ref[0, c], q_ref[ts, c * P_DHALF:(c + 1) * P_DHALF])
            v, i = _topk_rows(s, P_TOPK)
            sv.append(v)
            si.append(i)
        cand, cidx = _pair_candidates(sv, si)
        cv, eidx = _topk_rows(cand, P_TOPK, payload=cidx)
        e = jnp.exp(cv - cv[0:1])
        et_ref[rows, ts] = eidx
        gt_ref[rows, ts] = e / jnp.sum(e, axis=0, keepdims=True)

    @pl.when(h == P_HEADS - 1)
    def _():
        idx_ref[...] = et_ref[...].T.astype(I32)
        g_ref[...] = gt_ref[...].T


def _route(q, subkeys, tm):
    t = q.shape[0]
    out = pl.BlockSpec((tm, N_SEL), lambda i, h: (i, 0))
    return pl.pallas_call(
        _route_kernel,
        grid=(t // tm, P_HEADS),
        in_specs=[pl.BlockSpec((tm, 2 * P_DHALF), lambda i, h: (i, h)),
                  pl.BlockSpec((1, 2, P_NKEYS, P_DHALF), lambda i, h: (h, 0, 0, 0))],
        out_specs=[out, out],
        out_shape=[jax.ShapeDtypeStruct((t, N_SEL), I32), jax.ShapeDtypeStruct((t, N_SEL), F32)],
        scratch_shapes=[pltpu.VMEM((N_SEL, tm), F32), pltpu.VMEM((N_SEL, tm), F32)],
        compiler_params=_cparams("parallel", "arbitrary"),
        name="peer_route",
    )(q, subkeys)


def _load_row(tab_ref, e):
    return tab_ref[e].astype(F32)


def _sublane_merge(x, y, sh, mask):
    c = jnp.where(mask, x, y)
    d = jnp.where(mask, y, x)
    if 2 * sh == V7X_SUBLANES:
        return c + pltpu.roll(d, sh, axis=0)
    return c + jnp.where(mask, pltpu.roll(d, V7X_SUBLANES - sh, axis=0), pltpu.roll(d, sh, axis=0))


def _stage_blocks(i, nblk, tab_hbm, tab_vmem, per_block, sems):
    slot = i % 2

    def copies(blk, sl):
        return [pltpu.make_async_copy(h.at[blk], s.at[pl.ds(sl * h.shape[1], h.shape[1])],
                                      sems.at[1 + 2 * n + sl])
                for n, (h, s) in enumerate(per_block)]

    @pl.when(i == 0)
    def _():
        table = pltpu.make_async_copy(tab_hbm, tab_vmem, sems.at[0])
        table.start()
        for c in copies(0, 0):
            c.start()
        table.wait()

    for c in copies(i, slot):
        c.wait()

    @pl.when(i + 1 < nblk)
    def _():
        for c in copies(i + 1, 1 - slot):
            c.start()

    return slot


def _peer_up_kernel(x_ref, g_ref, idx_hbm, tab_hbm, w_ref, tab_vmem, idx_smem, part_ref, sems):
    i = pl.program_id(0)
    slot = _stage_blocks(i, pl.num_programs(0), tab_hbm, tab_vmem, [(idx_hbm, idx_smem)], sems)
    tb = x_ref.shape[0]
    chunk = part_ref.shape[0] // N_SEL
    sub = lax.broadcasted_iota(I32, (V7X_SUBLANES, V7X_LANES), 0)
    masks = {sh: (sub & sh) == 0 for sh in (4, 2, 1)}
    ones = jnp.ones((V7X_LANES, V7X_LANES), BF16)
    diag = (lax.broadcasted_iota(I32, (1, N_SEL, V7X_LANES), 1)
            == lax.broadcasted_iota(I32, (1, N_SEL, V7X_LANES), 2))

    def token_chunk(c, carry):
        def token(tt, carry):
            t = c * chunk + tt
            xt = x_ref[t]
            tok_base = (slot * tb + t) * N_SEL

            def group(j, carry):
                col = pl.multiple_of(j * PEER_GROUP, PEER_GROUP)
                for k in range(PEER_GROUP // V7X_SUBLANES):
                    first = col + k * V7X_SUBLANES
                    parts = [_load_row(tab_vmem, idx_smem[tok_base + first + r]) * xt
                             for r in range(V7X_SUBLANES)]
                    for sh in (4, 2, 1):
                        parts = [_sublane_merge(parts[a], parts[a + sh], sh, masks[sh])
                                 for a in range(len(parts)) if (a & sh) == 0]
                    part_ref[pl.ds(pl.multiple_of(tt * N_SEL + first, V7X_SUBLANES), V7X_SUBLANES), :] = parts[0]
                return carry

            return lax.fori_loop(0, N_SEL // PEER_GROUP, group, carry)

        lax.fori_loop(0, chunk, token, 0)
        part = part_ref[...]
        hi = part.astype(BF16)
        lo = (part - hi.astype(F32)).astype(BF16)
        tot = (_dot(hi, ones) + _dot(lo, ones)).reshape(chunk, N_SEL, V7X_LANES)
        s = jnp.sum(jnp.where(diag, tot, 0.0), axis=1)
        act = 0.5 * s * (1.0 + lax.erf(s * (2.0 ** -0.5)))
        rows = pl.ds(pl.multiple_of(c * chunk, chunk), chunk)
        w_ref[rows, :] = g_ref[rows, :] * act
        return carry

    lax.fori_loop(0, tb // chunk, token_chunk, 0)


def _peer_down_kernel(w_ref, idx_hbm, tab_hbm, f_ref, tab_vmem, idx_smem, sems):
    i = pl.program_id(0)
    slot = _stage_blocks(i, pl.num_programs(0), tab_hbm, tab_vmem, [(idx_hbm, idx_smem)], sems)
    tb = f_ref.shape[0]
    n_acc = 4
    eye = lax.broadcasted_iota(I32, (N_SEL, V7X_LANES), 0) == lax.broadcasted_iota(I32, (N_SEL, V7X_LANES), 1)
    ones = jnp.ones((V7X_LANES, V7X_LANES), BF16)

    def splat_rows(t):
        d = jnp.where(eye, jnp.broadcast_to(w_ref[t], (N_SEL, V7X_LANES)), 0.0)
        hi = d.astype(BF16)
        lo = (d - hi.astype(F32)).astype(BF16)
        return _dot(hi, ones) + _dot(lo, ones)

    def token(t, wl):
        wl_next = splat_rows(jnp.minimum(t + 1, tb - 1))
        tok_base = (slot * tb + t) * N_SEL
        accs = [None] * n_acc
        for r in range(N_SEL):
            term = wl[r:r + 1, :] * _load_row(tab_vmem, idx_smem[tok_base + r])
            accs[r % n_acc] = term if accs[r % n_acc] is None else accs[r % n_acc] + term
        f_ref[t] = (accs[0] + accs[1]) + (accs[2] + accs[3])
        return wl_next

    lax.fori_loop(0, tb, token, splat_rows(0))


def _peer_experts(x1, g, idx, u_tab, v_tab, tb):
    t, d = x1.shape
    nblk = t // tb
    x3 = x1.reshape(t, V7X_SUBLANES, d // V7X_SUBLANES)
    idx = idx.reshape(nblk, tb * N_SEL)
    any_spec = pl.BlockSpec(memory_space=pl.ANY)
    w = pl.pallas_call(
        _peer_up_kernel,
        grid=(nblk,),
        in_specs=[pl.BlockSpec((tb,) + x3.shape[1:], lambda i: (i, 0, 0)),
                  pl.BlockSpec((tb, N_SEL), lambda i: (i, 0)), any_spec, any_spec],
        out_specs=pl.BlockSpec((tb, N_SEL), lambda i: (i, 0)),
        out_shape=jax.ShapeDtypeStruct((t, N_SEL), F32),
        scratch_shapes=[pltpu.VMEM(u_tab.shape, u_tab.dtype), pltpu.SMEM((2 * tb * N_SEL,), I32),
                        pltpu.VMEM((PEER_CHUNK * N_SEL, V7X_LANES), F32), pltpu.SemaphoreType.DMA((3,))],
        compiler_params=_cparams("arbitrary"),
        name="peer_up",
    )(x3, g, idx, u_tab)
    f3 = pl.pallas_call(
        _peer_down_kernel,
        grid=(nblk,),
        in_specs=[pl.BlockSpec((tb, 1, N_SEL), lambda i: (i, 0, 0)), any_spec, any_spec],
        out_specs=pl.BlockSpec((tb,) + x3.shape[1:], lambda i: (i, 0, 0)),
        out_shape=jax.ShapeDtypeStruct(x3.shape, F32),
        scratch_shapes=[pltpu.VMEM(v_tab.shape, v_tab.dtype), pltpu.SMEM((2 * tb * N_SEL,), I32),
                        pltpu.SemaphoreType.DMA((3,))],
        compiler_params=_cparams("arbitrary"),
        name="peer_down",
    )(w.reshape(t, 1, N_SEL), idx, v_tab)
    return f3.reshape(t, d)


def _final_kernel(base_ref, f_ref, g_ref, b_ref, o_ref):
    o_ref[...] = _layer_norm(base_ref[...] + f_ref[...], g_ref[...], b_ref[...])


def _final(base, f, g2, b2):
    t, d = base.shape
    tm = min(ROW_TILE, t)
    row = pl.BlockSpec((tm, d), lambda i: (i, 0))
    vec = pl.BlockSpec((1, d), lambda i: (0, 0))
    return pl.pallas_call(
        _final_kernel,
        grid=(t // tm,),
        in_specs=[row, row, vec, vec],
        out_specs=row,
        out_shape=jax.ShapeDtypeStruct((t, d), F32),
        compiler_params=_cparams("parallel"),
        name="residual_ln2",
    )(base, f, g2, b2)


def _reorder_in_proj(w, b):
    hk = M_HEADS * M_DK
    widths = (hk, hk, M_HEADS * M_DV, M_HEADS * M_DV, M_HEADS, M_HEADS,
              A_HEADS * 2 * A_DK, A_HEADS * 2 * A_DK, A_HEADS * A_DV)
    cuts = [0]
    for n in widths:
        cuts.append(cuts[-1] + n)
    total = w.shape[1]
    d_model = (total - cuts[-1]) // 2
    cuts += [cuts[-1] + d_model, total]
    piece = lambda a, i: a[..., cuts[i]:cuts[i + 1]]
    order = (0, 1, 2, 3, 6, 7, 8, 9, 10, 4, 5)
    pad = V7X_LANES - 2 * M_HEADS
    wr = jnp.concatenate([piece(w, i) for i in order] + [jnp.zeros((w.shape[0], pad), w.dtype)], axis=1)
    br = jnp.concatenate([piece(b, i) for i in order] + [jnp.zeros((pad,), b.dtype)], axis=0)
    return wr.astype(BF16), br.reshape(1, -1)


def _table_tiles(tab):
    n, d = tab.shape
    return tab.astype(BF16).reshape(n, V7X_SUBLANES, d // V7X_SUBLANES)


def _token_local(x2, ya, yb, sga, sgb, p2, lw, alpha):
    t = x2.shape[0]
    if t % PEER_TILE:
        pad = lambda a: jnp.pad(a, ((0, PEER_TILE - t % PEER_TILE), (0, 0)))
        return _token_local(pad(x2), pad(ya), pad(yb), pad(sga), pad(sgb), pad(p2), lw, alpha)[:t]
    x1, base, q = _merge(x2, ya, yb, sga, sgb, p2, lw["wa"], lw["wb"], lw["wo"], lw["g1"], lw["b1"],
                         lw["wg"], lw["wp"], lw["wq"], alpha)
    idx, g = _route(q, lw["subkeys"], ROUTE_TILE if t % ROUTE_TILE == 0 else PEER_TILE)
    f = _peer_experts(x1, g, idx, lw["u_tab"], lw["v_tab"], PEER_TILE)
    return _final(base, f, lw["g2"], lw["b2"])


def kernel(x_prompt, x_sample, cache_k, cache_v, state_C, state_n, state_m, page_table, p_prompt, p_sample,
           w_in, b_in, lambda_q1, lambda_k1, lambda_q2, lambda_k2, mlstm_norm_g, diff_norm_g, w_branch_a,
           w_branch_b, w_out, ln1_g, ln1_b, peer_wq, peer_subkeys, peer_u, peer_v, ple_w_gate, ple_w_proj,
           ln2_g, ln2_b):
    depth = w_in.shape[0]
    batch, seq, d_model = x_prompt.shape
    db, dseq, _ = x_sample.shape
    assert dseq == 1, "the decode path handles one new token per request"
    alpha = (2.0 * depth) ** 0.25
    slopes = 2.0 ** (-8.0 * (jnp.arange(A_HEADS, dtype=F32) + 1.0) / A_HEADS)
    slopes_h = jnp.broadcast_to(slopes[:, None, None], (A_HEADS, 1, V7X_LANES))
    slopes_8 = jnp.broadcast_to(jnp.repeat(slopes, 2)[:, None], (2 * A_HEADS, V7X_LANES))

    hp = x_prompt.reshape(batch * seq, d_model)
    hs = x_sample.reshape(db, d_model)
    outs = [[] for _ in range(10)]
    for l in range(depth):
        lam_init = 0.8 - 0.6 * math.exp(-0.3 * l)
        w_l, b_l = _reorder_in_proj(w_in[l], b_in[l])
        row = lambda a: a.reshape(1, -1)
        lw = dict(wa=w_branch_a[l].astype(BF16), wb=w_branch_b[l].astype(BF16), wo=w_out[l].astype(BF16),
                  g1=row(ln1_g[l]), b1=row(ln1_b[l]), wg=ple_w_gate[l].astype(BF16),
                  wp=ple_w_proj[l].astype(BF16), wq=peer_wq[l].astype(BF16),
                  subkeys=peer_subkeys[l].astype(BF16), u_tab=_table_tiles(peer_u[l]),
                  v_tab=_table_tiles(peer_v[l]), g2=row(ln2_g[l]), b2=row(ln2_b[l]))
        lams = [row(a[l].astype(F32)) for a in (lambda_q1, lambda_k1, lambda_q2, lambda_k2)]
        mng = row(mlstm_norm_g[l])
        dng = row(diff_norm_g[l])

        mq, mk, mv, so, aq, akt, avf, akb, avb, sga, sgb, gates = _project(hp, w_l, b_l, seq=seq)
        ya, c_p, n_p, m_p = _mlstm_prompt(mq, mk, mv, so, gates, mng, batch, seq)
        yb = _attn_prompt(aq, akb, avb, slopes_h, dng, lams, batch, seq, lam_init)
        hp = _token_local(hp, ya, yb, sga, sgb, p_prompt[l].reshape(batch * seq, -1), lw, alpha)
        outs[0].append(jnp.transpose(akt.reshape(batch, A_HEADS, 2, A_DK, seq), (0, 4, 1, 2, 3)))
        outs[1].append(avf.reshape(batch, seq, A_HEADS, A_DV))
        outs[2].append(c_p)
        outs[3].append(n_p)
        outs[4].append(m_p[:, 0, :M_HEADS])

        mq, mk, mv, so, aq, akf, avf, akb, avb, sga, sgb, gates = _project(hs, w_l, b_l)
        m0 = jnp.pad(state_m[l].astype(F32), ((0, 0), (0, V7X_LANES - M_HEADS))).reshape(db, 1, V7X_LANES)
        ya, c_s, n_s, m_s = _mlstm_step(mq, mk, mv, so, gates, mng, state_C[l].astype(F32),
                                        state_n[l].astype(F32), m0)
        yb = _attn_decode(page_table, aq, akb, avb, cache_k[l], cache_v[l], slopes_8, dng, lams, lam_init)
        hs = _token_local(hs, ya.reshape(db, -1), yb, sga, sgb, p_sample[l].reshape(db, -1), lw, alpha)
        outs[5].append(akf.reshape(db, 1, A_HEADS, 2, A_DK))
        outs[6].append(avf.reshape(db, 1, A_HEADS, A_DV))
        outs[7].append(c_s)
        outs[8].append(n_s)
        outs[9].append(m_s[:, 0, :M_HEADS])

    st = [jnp.stack(o, 0) for o in outs]
    return (hp.reshape(batch, seq, d_model), hs.reshape(db, 1, d_model),
            st[0], st[1], st[2], st[3], st[4], st[5], st[6], st[7], st[8], st[9])
```

```python
import functools
import math

import jax
import jax.numpy as jnp
from jax import lax
from jax.experimental import pallas as pl
from jax.experimental.pallas import tpu as pltpu

F32 = jnp.float32
BF16 = jnp.bfloat16
I32 = jnp.int32

M_HEADS, M_DK, M_DV = 4, 128, 128
A_HEADS, A_DK, A_DV = 4, 64, 128
P_HEADS, P_NKEYS, P_DHALF, P_TOPK = 8, 128, 128, 16
PAGE_SIZE = 128
LN_EPS = 1e-5
N_SEL = P_HEADS * P_TOPK

V7X_LANES = 128
V7X_SUBLANES = 8
V7X_VMEM_LIMIT = 56 * 1024 * 1024

MLSTM_CHUNK = 128
ATTN_BLOCK = 512
PAGES_PER_STEP = 16
ROW_TILE = 256
PEER_TILE = 128
PEER_GROUP = 128
PEER_CHUNK = 32
ROUTE_TILE = 256

NEG_INF = float("-inf")


def _cparams(*sem):
    return pltpu.CompilerParams(dimension_semantics=sem, vmem_limit_bytes=V7X_VMEM_LIMIT)


def _nt_dot(a, b):
    return lax.dot_general(a, b, (((1,), (1,)), ((), ())), preferred_element_type=F32)


def _dot(a, b):
    return jnp.dot(a, b, preferred_element_type=F32)


_C_MQ, _C_MK, _C_MV, _C_MO, _C_AQ, _C_AK, _C_AV, _C_GA = 0, 512, 1024, 1536, 2048, 2560, 3072, 3584
_C_GB, _C_GATE, _C_END = 4608, 5632, 5760


def _proj_kernel(x_ref, w_ref, b_ref, mq, mk, mv, so, aq, akf, avf, akb, avb, sga, sgb, gates, *, k_by_position):
    xb = x_ref[...].astype(BF16)

    def seg(lo, hi):
        return _dot(xb, w_ref[:, lo:hi]) + b_ref[:, lo:hi]

    mq[...] = seg(_C_MQ, _C_MK).astype(BF16)
    mk[...] = (seg(_C_MK, _C_MV) * (M_DK ** -0.5)).astype(BF16)
    mv[...] = seg(_C_MV, _C_MO).astype(BF16)
    so[...] = jax.nn.sigmoid(seg(_C_MO, _C_AQ)).astype(BF16)
    aq[...] = (seg(_C_AQ, _C_AK) * (A_DK ** -0.5)).astype(BF16)
    k = seg(_C_AK, _C_AV)
    if k_by_position:
        akf[0] = k.T
    else:
        akf[...] = k
    akb[...] = k.astype(BF16)
    v = seg(_C_AV, _C_GA)
    avf[...] = v
    avb[...] = v.astype(BF16)
    sga[...] = jax.nn.sigmoid(seg(_C_GA, _C_GB)).astype(BF16)
    sgb[...] = jax.nn.sigmoid(seg(_C_GB, _C_GATE)).astype(BF16)
    gates[...] = seg(_C_GATE, _C_END)


def _project(x2, w, b, seq=None):
    t, d = x2.shape
    tm = min(ROW_TILE, t)
    row = lambda i: (i, 0)
    fixed = lambda i: (0, 0)
    widths = [(512, BF16)] * 5 + [(512, F32)] * 2 + [(512, BF16)] * 2 + [(1024, BF16)] * 2 + [(128, F32)]
    out_specs = [pl.BlockSpec((tm, n), row) for n, _ in widths]
    out_shape = [jax.ShapeDtypeStruct((t, n), dt) for n, dt in widths]
    if seq is not None:
        per_seq = seq // tm
        out_specs[5] = pl.BlockSpec((1, widths[5][0], tm), lambda i: (i // per_seq, 0, i % per_seq))
        out_shape[5] = jax.ShapeDtypeStruct((t // seq, widths[5][0], seq), F32)
    return pl.pallas_call(
        functools.partial(_proj_kernel, k_by_position=seq is not None),
        grid=(t // tm,),
        in_specs=[pl.BlockSpec((tm, d), row), pl.BlockSpec(w.shape, fixed), pl.BlockSpec(b.shape, fixed)],
        out_specs=out_specs,
        out_shape=out_shape,
        compiler_params=_cparams("parallel"),
        name="project_in",
    )(x2, w, b)


def _head_ln_gate(hh, g_row, so):
    mu = jnp.mean(hh, axis=-1, keepdims=True)
    var = jnp.mean(jnp.square(hh - mu), axis=-1, keepdims=True)
    return ((hh - mu) * lax.rsqrt(var + LN_EPS) * g_row * so.astype(F32)).astype(BF16)


def _mlstm_prompt_kernel(q_ref, k_ref, v_ref, so_ref, g_ref, ng_ref, ya_ref, c_ref, n_ref, m_ref):
    @pl.when(pl.program_id(1) == 0)
    def _():
        c_ref[...] = jnp.zeros_like(c_ref)
        n_ref[...] = jnp.zeros_like(n_ref)
        m_ref[...] = jnp.zeros_like(m_ref)

    L = q_ref.shape[0]
    gate = g_ref[...]
    cum = jax.nn.log_sigmoid(gate)
    row = lax.broadcasted_iota(I32, cum.shape, 0)
    sh = 1
    while sh < L:
        cum = cum + jnp.where(row >= sh, pltpu.roll(cum, sh, axis=0), 0.0)
        sh *= 2
    cum_t = cum.T
    gate_t = gate.T
    causal = lax.broadcasted_iota(I32, (L, L), 1) <= lax.broadcasted_iota(I32, (L, L), 0)
    lane = lax.broadcasted_iota(I32, (1, V7X_LANES), 1)
    m_all = m_ref[0]
    m_next = m_all
    for h in range(M_HEADS):
        hs = slice(h * M_DK, (h + 1) * M_DK)
        b_col = cum[:, M_HEADS + h:M_HEADS + h + 1]
        b_row = cum_t[M_HEADS + h:M_HEADS + h + 1, :]
        i_col = gate[:, h:h + 1]
        i_row = gate_t[h:h + 1, :]
        m0 = m_all[:, h:h + 1]
        dmat = jnp.where(causal, b_col - b_row + i_row, NEG_INF)
        inter = b_col + m0
        m = jnp.maximum(inter, jnp.max(dmat, axis=1, keepdims=True))
        w = jnp.exp(dmat - m)
        a = jnp.exp(inter - m)
        qh, kh, vh = q_ref[:, hs], k_ref[:, hs], v_ref[:, hs]
        c0 = c_ref[0, h]
        n0 = n_ref[0, h:h + 1, :]
        qk = _nt_dot(qh, kh) * w
        num = _dot(qk.astype(BF16), vh) + a * _dot(qh, c0.astype(BF16))
        den = jnp.sum(qk, axis=1, keepdims=True) + a * jnp.sum(qh.astype(F32) * n0, axis=1, keepdims=True)
        hh = num / jnp.maximum(jnp.abs(den), jnp.exp(-m))
        ya_ref[:, hs] = _head_ln_gate(hh, ng_ref[:, hs], so_ref[:, hs])
        m_last = m[L - 1:L, :]
        b_last = b_col[L - 1:L, :]
        w_last = jnp.exp(b_last - b_col + i_col - m_last)
        a_last = jnp.exp(b_last + m0 - m_last)
        kw = kh.astype(F32) * w_last
        c_ref[0, h] = a_last * c0 + _dot(kw.T.astype(BF16), vh)
        n_ref[0, h:h + 1, :] = a_last * n0 + jnp.sum(kw, axis=0, keepdims=True)
        m_next = jnp.where(lane == h, m_last, m_next)
    m_ref[0] = m_next


def _mlstm_prompt(mq, mk, mv, so, gates, ng, batch, seq):
    L = min(MLSTM_CHUNK, seq)
    nc = seq // L
    tok = lambda b, c: (b * nc + c, 0)
    wide = pl.BlockSpec((L, M_HEADS * M_DK), tok)
    return pl.pallas_call(
        _mlstm_prompt_kernel,
        grid=(batch, nc),
        in_specs=[wide, wide, wide, wide, pl.BlockSpec((L, V7X_LANES), tok),
                  pl.BlockSpec((1, M_HEADS * M_DV), lambda b, c: (0, 0))],
        out_specs=[wide,
                   pl.BlockSpec((1, M_HEADS, M_DK, M_DV), lambda b, c: (b, 0, 0, 0)),
                   pl.BlockSpec((1, M_HEADS, M_DK), lambda b, c: (b, 0, 0)),
                   pl.BlockSpec((1, 1, V7X_LANES), lambda b, c: (b, 0, 0))],
        out_shape=[jax.ShapeDtypeStruct((batch * seq, M_HEADS * M_DV), BF16),
                   jax.ShapeDtypeStruct((batch, M_HEADS, M_DK, M_DV), F32),
                   jax.ShapeDtypeStruct((batch, M_HEADS, M_DK), F32),
                   jax.ShapeDtypeStruct((batch, 1, V7X_LANES), F32)],
        compiler_params=_cparams("parallel", "arbitrary"),
        name="mlstm_prompt",
    )(mq, mk, mv, so, gates, ng)


def _mlstm_step_kernel(q_ref, k_ref, v_ref, so_ref, g_ref, ng_ref, c0_ref, n0_ref, m0_ref,
                       ya_ref, c_ref, n_ref, m_ref):
    gate = g_ref[0]
    m_all = m0_ref[0]
    lane = lax.broadcasted_iota(I32, (1, V7X_LANES), 1)
    diag = lax.broadcasted_iota(I32, (M_DK, M_DK), 0) == lax.broadcasted_iota(I32, (M_DK, M_DK), 1)
    m_next = m_all
    for h in range(M_HEADS):
        hs = slice(h * M_DK, (h + 1) * M_DK)
        q = q_ref[0][:, hs].astype(F32)
        k = k_ref[0][:, hs].astype(F32)
        v = v_ref[0][:, hs].astype(F32)
        ig = gate[:, h:h + 1]
        lf = jax.nn.log_sigmoid(gate[:, M_HEADS + h:M_HEADS + h + 1])
        m0 = m_all[:, h:h + 1]
        inter = lf + m0
        m = jnp.maximum(inter, ig)
        w = jnp.exp(ig - m)
        a = jnp.exp(inter - m)
        c0 = c0_ref[0, h]
        n0 = n0_ref[0, h:h + 1, :]
        qk = jnp.sum(q * k, axis=1, keepdims=True) * w
        q8 = jnp.broadcast_to(q, (V7X_SUBLANES, M_DK)).astype(BF16)
        qc = _dot(q8, c0.astype(BF16))[0:1, :]
        num = qk * v + a * qc
        den = qk + a * jnp.sum(q * n0, axis=1, keepdims=True)
        hh = num / jnp.maximum(jnp.abs(den), jnp.exp(-m))
        ya_ref[0, :, hs] = _head_ln_gate(hh, ng_ref[:, hs], so_ref[0][:, hs])
        kd = jnp.where(diag, jnp.broadcast_to(k, (M_DK, M_DK)), 0.0).astype(BF16)
        vb = jnp.broadcast_to(v, (M_DK, M_DV)).astype(BF16)
        c_ref[0, h] = a * c0 + w * _dot(kd, vb)
        n_ref[0, h:h + 1, :] = a * n0 + w * k
        m_next = jnp.where(lane == h, m, m_next)
    m_ref[0] = m_next


def _mlstm_step(mq, mk, mv, so, gates, ng, c0, n0, m0):
    db = mq.shape[0]
    r3 = lambda a: a.reshape(db, 1, a.shape[-1])
    tok = lambda b: (b, 0, 0)
    wide = pl.BlockSpec((1, 1, M_HEADS * M_DK), tok)
    narrow = pl.BlockSpec((1, 1, V7X_LANES), tok)
    cspec = pl.BlockSpec((1, M_HEADS, M_DK, M_DV), lambda b: (b, 0, 0, 0))
    nspec = pl.BlockSpec((1, M_HEADS, M_DK), tok)
    return pl.pallas_call(
        _mlstm_step_kernel,
        grid=(db,),
        in_specs=[wide, wide, wide, wide, narrow, pl.BlockSpec((1, M_HEADS * M_DV), lambda b: (0, 0)),
                  cspec, nspec, narrow],
        out_specs=[wide, cspec, nspec, narrow],
        out_shape=[jax.ShapeDtypeStruct((db, 1, M_HEADS * M_DV), BF16),
                   jax.ShapeDtypeStruct(c0.shape, F32),
                   jax.ShapeDtypeStruct(n0.shape, F32),
                   jax.ShapeDtypeStruct((db, 1, V7X_LANES), F32)],
        compiler_params=_cparams("parallel"),
        name="mlstm_step",
    )(r3(mq), r3(mk), r3(mv), r3(so), r3(gates), ng, c0, n0, m0)


def _lambda_value(lq1, lk1, lq2, lk2, lam_init):
    s1 = jnp.sum(lq1[...] * lk1[...], axis=1, keepdims=True)
    s2 = jnp.sum(lq2[...] * lk2[...], axis=1, keepdims=True)
    return jnp.exp(s1) - jnp.exp(s2) + lam_init


def _rms_gain(o, g_row, lam_init):
    return o * lax.rsqrt(jnp.mean(jnp.square(o), axis=-1, keepdims=True) + LN_EPS) * g_row * (1.0 - lam_init)


def _attn_prompt_kernel(q_ref, k_ref, v_ref, slope_ref, ng_ref, lq1, lk1, lq2, lk2, o_ref, *, lam_init):
    tq = q_ref.shape[0]
    qi = pl.program_id(2)
    q = q_ref[...]
    lane = lax.broadcasted_iota(I32, q.shape, 1)
    zero = jnp.zeros_like(q)
    qq = jnp.concatenate([jnp.where(lane < A_DK, q, zero), jnp.where(lane >= A_DK, q, zero)], axis=0)
    slope = slope_ref[0]
    slope_row = jnp.concatenate([slope] * (tq // V7X_LANES), axis=1)
    col = lax.broadcasted_iota(I32, (1, tq), 1)

    def block(j, carry, masked):
        m, l, acc = carry
        start = pl.multiple_of(j * tq, tq)
        kb = k_ref[pl.ds(start, tq), :]
        vb = v_ref[pl.ds(start, tq), :]
        s = _nt_dot(qq, kb) + slope_row * (j * tq + col).astype(F32)
        if masked:
            r = lax.broadcasted_iota(I32, s.shape, 0)
            r = jnp.where(r >= tq, r - tq, r)
            s = jnp.where(lax.broadcasted_iota(I32, s.shape, 1) <= r, s, NEG_INF)
        m_new = jnp.maximum(m, jnp.max(s, axis=1, keepdims=True))
        p = jnp.exp(s - m_new)
        alpha = jnp.exp(m - m_new)
        l = alpha * l + jnp.sum(p, axis=1, keepdims=True)
        acc = alpha * acc + _dot(p.astype(BF16), vb)
        return m_new, l, acc

    init = (jnp.full((2 * tq, 1), NEG_INF, F32), jnp.zeros((2 * tq, 1), F32), jnp.zeros((2 * tq, A_DV), F32))
    carry = lax.fori_loop(0, qi, lambda j, c: block(j, c, False), init)
    _, l, acc = block(qi, carry, True)
    lam = _lambda_value(lq1, lk1, lq2, lk2, lam_init)
    o = acc[:tq] / l[:tq] - lam * (acc[tq:] / l[tq:])
    o_ref[...] = _rms_gain(o, ng_ref[...], lam_init).astype(BF16)


def _attn_prompt(aq, akb, avb, slopes, ng, lams, batch, seq, lam_init):
    tq = min(ATTN_BLOCK, seq)
    nq = seq // tq
    hw = 2 * A_DK
    lam_spec = pl.BlockSpec((1, A_DK), lambda b, h, i: (0, 0))
    return pl.pallas_call(
        functools.partial(_attn_prompt_kernel, lam_init=lam_init),
        grid=(batch, A_HEADS, nq),
        in_specs=[pl.BlockSpec((tq, hw), lambda b, h, i: (b * nq + i, h)),
                  pl.BlockSpec((seq, hw), lambda b, h, i: (b, h)),
                  pl.BlockSpec((seq, A_DV), lambda b, h, i: (b, h)),
                  pl.BlockSpec((1, 1, V7X_LANES), lambda b, h, i: (h, 0, 0)),
                  pl.BlockSpec((1, A_DV), lambda b, h, i: (0, h)),
                  lam_spec, lam_spec, lam_spec, lam_spec],
        out_specs=pl.BlockSpec((tq, A_DV), lambda b, h, i: (b * nq + i, h)),
        out_shape=jax.ShapeDtypeStruct((batch * seq, A_HEADS * A_DV), BF16),
        compiler_params=_cparams("parallel", "parallel", "arbitrary"),
        name="diff_attn_prompt",
    )(aq, akb, avb, slopes, ng, *lams)


def _attn_decode_kernel(pt_ref, q_ref, ks_ref, vs_ref, slope_ref, ng_ref, lq1, lk1, lq2, lk2, *rest,
                        lam_init, pps):
    k_refs, v_refs = rest[:pps], rest[pps:2 * pps]
    o_ref, m_ref, l_ref, acc_ref = rest[2 * pps:]
    j = pl.program_id(1)
    nmap = 2 * A_HEADS
    width = A_HEADS * 2 * A_DK

    @pl.when(j == 0)
    def _():
        m_ref[...] = jnp.full(m_ref.shape, NEG_INF, F32)
        l_ref[...] = jnp.zeros_like(l_ref)
        acc_ref[...] = jnp.zeros_like(acc_ref)

    r8 = lax.broadcasted_iota(I32, (nmap, width), 0)
    c8 = lax.broadcasted_iota(I32, (nmap, width), 1)
    own = (c8 // A_DK) == r8
    qmat = jnp.where(own, jnp.broadcast_to(q_ref[0].astype(F32), (nmap, width)), 0.0)
    qb = qmat.astype(BF16)
    slope = slope_ref[...]
    lane = lax.broadcasted_iota(I32, (nmap, pps * PAGE_SIZE), 1)

    def fold(s):
        m_old = m_ref[...]
        m_new = jnp.maximum(m_old, jnp.max(s, axis=1, keepdims=True))
        alpha = jnp.exp(m_old - m_new)
        p = jnp.exp(s - m_new[:, 0:1])
        m_ref[...] = m_new
        return alpha, p

    q_maps = [jnp.broadcast_to(qb[r:r + 1, r * A_DK:(r + 1) * A_DK], (V7X_SUBLANES, A_DK)) for r in range(nmap)]
    s = jnp.concatenate(
        [jnp.concatenate([_dot(q_maps[r], k_refs[i][0, r // 2, r % 2].astype(BF16))[0:1] for r in range(nmap)],
                         axis=0) for i in range(pps)], axis=1)
    s = s + slope[:, 0:1] * (j * (pps * PAGE_SIZE) + lane).astype(F32)
    alpha, p = fold(s)
    l_ref[...] = alpha * l_ref[...] + jnp.sum(p, axis=1, keepdims=True)
    pv = []
    for h in range(A_HEADS):
        parts = [_dot(p[:, i * PAGE_SIZE:(i + 1) * PAGE_SIZE], v_refs[i][0, pl.ds(h, PAGE_SIZE, stride=A_HEADS), :])
                 for i in range(pps)]
        while len(parts) > 1:
            parts = [a + b for a, b in zip(parts[0::2], parts[1::2])]
        pv.append(parts[0])
    acc_ref[...] = alpha[:, 0:1] * acc_ref[...] + jnp.concatenate(pv, axis=1)

    @pl.when(j == pl.num_programs(1) - 1)
    def _():
        past = jnp.asarray(pl.num_programs(1) * pps * PAGE_SIZE, F32)
        k_new = ks_ref[0].astype(F32)
        v_new = vs_ref[0].astype(F32)
        s = jnp.sum(qmat * k_new, axis=1, keepdims=True) + slope[:, 0:1] * past
        alpha, p = fold(s)
        l = alpha * l_ref[...] + p
        acc = alpha[:, 0:1] * acc_ref[...] + p * v_new
        o = acc / l[:, 0:1]
        lam = _lambda_value(lq1, lk1, lq2, lk2, lam_init)
        d = o - lam * pltpu.roll(o, nmap - 1, axis=0)
        keep = r8 == 2 * (c8 // A_DV)
        hb = jnp.sum(jnp.where(keep, d, 0.0), axis=0, keepdims=True)
        for h in range(A_HEADS):
            hs = slice(h * A_DV, (h + 1) * A_DV)
            o_ref[0, :, hs] = _rms_gain(hb[:, hs], ng_ref[:, hs], lam_init).astype(BF16)


def _attn_decode(page_table, aq, akb, avb, cache_k, cache_v, slopes8, ng, lams, lam_init):
    db, n_pages = page_table.shape
    pps = math.gcd(PAGES_PER_STEP, n_pages)
    width = A_HEADS * 2 * A_DK
    r3 = lambda a: a.reshape(db, 1, a.shape[-1])
    tok = lambda b, j, pt: (b, 0, 0)
    fixed = lambda b, j, pt: (0, 0)
    cache_k = jnp.transpose(cache_k, (0, 2, 3, 4, 1))
    k_specs = [pl.BlockSpec((1,) + cache_k.shape[1:], lambda b, j, pt, i=i: (pt[b, j * pps + i], 0, 0, 0, 0))
               for i in range(pps)]
    cache_v = cache_v.reshape(cache_v.shape[0], PAGE_SIZE * A_HEADS, A_DV)
    v_specs = [pl.BlockSpec((1,) + cache_v.shape[1:], lambda b, j, pt, i=i: (pt[b, j * pps + i], 0, 0))
               for i in range(pps)]
    lam_spec = pl.BlockSpec((1, A_DK), fixed)
    grid_spec = pltpu.PrefetchScalarGridSpec(
        num_scalar_prefetch=1,
        grid=(db, n_pages // pps),
        in_specs=[pl.BlockSpec((1, 1, width), tok), pl.BlockSpec((1, 1, width), tok),
                  pl.BlockSpec((1, 1, width), tok),
                  pl.BlockSpec((2 * A_HEADS, V7X_LANES), fixed), pl.BlockSpec((1, width), fixed),
                  lam_spec, lam_spec, lam_spec, lam_spec] + k_specs + v_specs,
        out_specs=pl.BlockSpec((1, 1, width), tok),
        scratch_shapes=[pltpu.VMEM((2 * A_HEADS, V7X_LANES), F32), pltpu.VMEM((2 * A_HEADS, V7X_LANES), F32),
                        pltpu.VMEM((2 * A_HEADS, width), F32)],
    )
    out = pl.pallas_call(
        functools.partial(_attn_decode_kernel, lam_init=lam_init, pps=pps),
        grid_spec=grid_spec,
        out_shape=jax.ShapeDtypeStruct((db, 1, width), BF16),
        compiler_params=_cparams("parallel", "arbitrary"),
        name="diff_attn_decode",
    )(page_table, r3(aq), r3(akb), r3(avb), slopes8, ng, *lams, *([cache_k] * pps), *([cache_v] * pps))
    return out.reshape(db, width)


def _layer_norm(x, g, b):
    mu = jnp.mean(x, axis=-1, keepdims=True)
    var = jnp.mean(jnp.square(x - mu), axis=-1, keepdims=True)
    return (x - mu) * lax.rsqrt(var + LN_EPS) * g + b


def _merge_kernel(x_ref, ya_ref, yb_ref, sga_ref, sgb_ref, p_ref, wa_ref, wb_ref, wo_ref, g1_ref, b1_ref,
                  wg_ref, wp_ref, wq_ref, x1_ref, base_ref, q_ref, *, alpha):
    ya = _dot(ya_ref[...], wa_ref[...])
    yb = _dot(yb_ref[...], wb_ref[...])
    y = sga_ref[...].astype(F32) * ya + sgb_ref[...].astype(F32) * yb
    y = _dot(y.astype(BF16), wo_ref[...])
    x1 = _layer_norm(alpha * x_ref[...] + y, g1_ref[...], b1_ref[...])
    x1_ref[...] = x1
    x1b = x1.astype(BF16)
    pe = jax.nn.sigmoid(_dot(x1b, wg_ref[...])) * _dot(p_ref[...].astype(BF16), wp_ref[...])
    base_ref[...] = alpha * x1 + pe
    q_ref[...] = _dot(x1b, wq_ref[...]).astype(BF16)


def _merge(x2, ya, yb, sga, sgb, p2, wa, wb, wo, g1, b1, wg, wp, wq, alpha):
    t, d = x2.shape
    tm = min(ROW_TILE, t)
    row = lambda i: (i, 0)
    fixed = lambda i: (0, 0)
    rows = lambda a: pl.BlockSpec((tm, a.shape[1]), row)
    full = lambda a: pl.BlockSpec(a.shape, fixed)
    nq = wq.shape[1]
    return pl.pallas_call(
        functools.partial(_merge_kernel, alpha=alpha),
        grid=(t // tm,),
        in_specs=[rows(x2), rows(ya), rows(yb), rows(sga), rows(sgb), rows(p2),
                  full(wa), full(wb), full(wo), full(g1), full(b1), full(wg), full(wp), full(wq)],
        out_specs=[pl.BlockSpec((tm, d), row), pl.BlockSpec((tm, d), row), pl.BlockSpec((tm, nq), row)],
        out_shape=[jax.ShapeDtypeStruct((t, d), F32), jax.ShapeDtypeStruct((t, d), F32),
                   jax.ShapeDtypeStruct((t, nq), BF16)],
        compiler_params=_cparams("parallel"),
        name="merge_ln1",
    )(x2, ya, yb, sga, sgb, p2, wa, wb, wo, g1, b1, wg, wp, wq)


def _topk_rows(s, k, payload=None):
    n = s.shape[0]
    rows = lax.broadcasted_iota(I32, s.shape, 0).astype(F32)
    vals, picks = [], []
    for _ in range(k):
        mx = jnp.max(s, axis=0, keepdims=True)
        idx = jnp.min(jnp.where(s == mx, rows, float(n)), axis=0, keepdims=True)
        sel = rows == idx
        if payload is None:
            picks.append(idx)
        else:
            picks.append(jnp.sum(jnp.where(sel, payload, 0.0), axis=0, keepdims=True))
        vals.append(mx)
        s = jnp.where(sel, NEG_INF, s)
    return jnp.concatenate(vals, axis=0), jnp.concatenate(picks, axis=0)


def _pair_candidates(sv, si):
    sub = lax.broadcasted_iota(I32, (V7X_SUBLANES,) + sv[0].shape[1:], 0)
    pair = lambda x, y, mul: x * mul + y
    vals = [sv[0][0:1] + sv[1], sv[0][1:2] + sv[1][0:V7X_SUBLANES]]
    idxs = [pair(si[0][0:1], si[1], float(P_NKEYS)), pair(si[0][1:2], si[1][0:V7X_SUBLANES], float(P_NKEYS))]
    for a in range(2, V7X_SUBLANES):
        keep = sub < P_TOPK // (a + 1)
        vals.append(jnp.where(keep, sv[0][a:a + 1] + sv[1][0:V7X_SUBLANES], NEG_INF))
        idxs.append(pair(si[0][a:a + 1], si[1][0:V7X_SUBLANES], float(P_NKEYS)))
    vals.append(sv[0][V7X_SUBLANES:] + sv[1][0:1])
    idxs.append(pair(si[0][V7X_SUBLANES:], si[1][0:1], float(P_NKEYS)))
    return jnp.concatenate(vals, axis=0), jnp.concatenate(idxs, axis=0)


def _route_kernel(q_ref, sk_ref, idx_ref, g_ref, et_ref, gt_ref):
    h = pl.program_id(1)
    rows = pl.ds(pl.multiple_of(h * P_TOPK, P_TOPK), P_TOPK)
    for part in range(q_ref.shape[0] // V7X_LANES):
        ts = slice(part * V7X_LANES, (part + 1) * V7X_LANES)
        sv, si = [], []
        for c in range(2):
            s = _nt_dot(sk_ref[0, c], q_ref[ts, c * P_DHALF:(c + 1) * P_DHALF])
            v, i = _topk_rows(s, P_TOPK)
            sv.append(v)
            si.append(i)
        cand, cidx = _pair_candidates(sv, si)
        cv, eidx = _topk_rows(cand, P_TOPK, payload=cidx)
        e = jnp.exp(cv - cv[0:1])
        et_ref[rows, ts] = eidx
        gt_ref[rows, ts] = e / jnp.sum(e, axis=0, keepdims=True)

    @pl.when(h == P_HEADS - 1)
    def _():
        idx_ref[...] = et_ref[...].T.astype(I32)
        g_ref[...] = gt_ref[...].T


def _route(q, subkeys, tm):
    t = q.shape[0]
    out = pl.BlockSpec((tm, N_SEL), lambda i, h: (i, 0))
    return pl.pallas_call(
        _route_kernel,
        grid=(t // tm, P_HEADS),
        in_specs=[pl.BlockSpec((tm, 2 * P_DHALF), lambda i, h: (i, h)),
                  pl.BlockSpec((1, 2, P_NKEYS, P_DHALF), lambda i, h: (h, 0, 0, 0))],
        out_specs=[out, out],
        out_shape=[jax.ShapeDtypeStruct((t, N_SEL), I32), jax.ShapeDtypeStruct((t, N_SEL), F32)],
        scratch_shapes=[pltpu.VMEM((N_SEL, tm), F32), pltpu.VMEM((N_SEL, tm), F32)],
        compiler_params=_cparams("parallel", "arbitrary"),
        name="peer_route",
    )(q, subkeys)


def _load_row(tab_ref, e):
    return tab_ref[e].astype(F32)


def _sublane_merge(x, y, sh, mask):
    c = jnp.where(mask, x, y)
    d = jnp.where(mask, y, x)
    if 2 * sh == V7X_SUBLANES:
        return c + pltpu.roll(d, sh, axis=0)
    return c + jnp.where(mask, pltpu.roll(d, V7X_SUBLANES - sh, axis=0), pltpu.roll(d, sh, axis=0))


def _stage_blocks(i, nblk, tab_hbm, tab_vmem, per_block, sems):
    slot = i % 2

    def copies(blk, sl):
        return [pltpu.make_async_copy(h.at[blk], s.at[pl.ds(sl * h.shape[1], h.shape[1])],
                                      sems.at[1 + 2 * n + sl])
                for n, (h, s) in enumerate(per_block)]

    @pl.when(i == 0)
    def _():
        table = pltpu.make_async_copy(tab_hbm, tab_vmem, sems.at[0])
        table.start()
        for c in copies(0, 0):
            c.start()
        table.wait()

    for c in copies(i, slot):
        c.wait()

    @pl.when(i + 1 < nblk)
    def _():
        for c in copies(i + 1, 1 - slot):
            c.start()

    return slot


def _peer_up_kernel(x_ref, g_ref, idx_hbm, tab_hbm, w_ref, tab_vmem, idx_smem, part_ref, sems):
    i = pl.program_id(0)
    slot = _stage_blocks(i, pl.num_programs(0), tab_hbm, tab_vmem, [(idx_hbm, idx_smem)], sems)
    tb = x_ref.shape[0]
    chunk = part_ref.shape[0] // N_SEL
    sub = lax.broadcasted_iota(I32, (V7X_SUBLANES, V7X_LANES), 0)
    masks = {sh: (sub & sh) == 0 for sh in (4, 2, 1)}
    ones = jnp.ones((V7X_LANES, V7X_LANES), BF16)
    diag = (lax.broadcasted_iota(I32, (1, N_SEL, V7X_LANES), 1)
            == lax.broadcasted_iota(I32, (1, N_SEL, V7X_LANES), 2))

    def token_chunk(c, carry):
        def token(tt, carry):
            t = c * chunk + tt
            xt = x_ref[t]
            tok_base = (slot * tb + t) * N_SEL

            def group(j, carry):
                col = pl.multiple_of(j * PEER_GROUP, PEER_GROUP)
                for k in range(PEER_GROUP // V7X_SUBLANES):
                    first = col + k * V7X_SUBLANES
                    parts = [_load_row(tab_vmem, idx_smem[tok_base + first + r]) * xt
                             for r in range(V7X_SUBLANES)]
                    for sh in (4, 2, 1):
                        parts = [_sublane_merge(parts[a], parts[a + sh], sh, masks[sh])
                                 for a in range(len(parts)) if (a & sh) == 0]
                    part_ref[pl.ds(pl.multiple_of(tt * N_SEL + first, V7X_SUBLANES), V7X_SUBLANES), :] = parts[0]
                return carry

            return lax.fori_loop(0, N_SEL // PEER_GROUP, group, carry)

        lax.fori_loop(0, chunk, token, 0)
        part = part_ref[...]
        hi = part.astype(BF16)
        lo = (part - hi.astype(F32)).astype(BF16)
        tot = (_dot(hi, ones) + _dot(lo, ones)).reshape(chunk, N_SEL, V7X_LANES)
        s = jnp.sum(jnp.where(diag, tot, 0.0), axis=1)
        act = 0.5 * s * (1.0 + lax.erf(s * (2.0 ** -0.5)))
        rows = pl.ds(pl.multiple_of(c * chunk, chunk), chunk)
        w_ref[rows, :] = g_ref[rows, :] * act
        return carry

    lax.fori_loop(0, tb // chunk, token_chunk, 0)


def _peer_down_kernel(w_ref, idx_hbm, tab_hbm, f_ref, tab_vmem, idx_smem, sems):
    i = pl.program_id(0)
    slot = _stage_blocks(i, pl.num_programs(0), tab_hbm, tab_vmem, [(idx_hbm, idx_smem)], sems)
    tb = f_ref.shape[0]
    n_acc = 4
    eye = lax.broadcasted_iota(I32, (N_SEL, V7X_LANES), 0) == lax.broadcasted_iota(I32, (N_SEL, V7X_LANES), 1)
    ones = jnp.ones((V7X_LANES, V7X_LANES), BF16)

    def splat_rows(t):
        d = jnp.where(eye, jnp.broadcast_to(w_ref[t], (N_SEL, V7X_LANES)), 0.0)
        hi = d.astype(BF16)
        lo = (d - hi.astype(F32)).astype(BF16)
        return _dot(hi, ones) + _dot(lo, ones)

    def token(t, wl):
        wl_next = splat_rows(jnp.minimum(t + 1, tb - 1))
        tok_base = (slot * tb + t) * N_SEL
        accs = [None] * n_acc
        for r in range(N_SEL):
            term = wl[r:r + 1, :] * _load_row(tab_vmem, idx_smem[tok_base + r])
            accs[r % n_acc] = term if accs[r % n_acc] is None else accs[r % n_acc] + term
        f_ref[t] = (accs[0] + accs[1]) + (accs[2] + accs[3])
        return wl_next

    lax.fori_loop(0, tb, token, splat_rows(0))


def _peer_experts(x1, g, idx, u_tab, v_tab, tb):
    t, d = x1.shape
    nblk = t // tb
    x3 = x1.reshape(t, V7X_SUBLANES, d // V7X_SUBLANES)
    idx = idx.reshape(nblk, tb * N_SEL)
    any_spec = pl.BlockSpec(memory_space=pl.ANY)
    w = pl.pallas_call(
        _peer_up_kernel,
        grid=(nblk,),
        in_specs=[pl.BlockSpec((tb,) + x3.shape[1:], lambda i: (i, 0, 0)),
                  pl.BlockSpec((tb, N_SEL), lambda i: (i, 0)), any_spec, any_spec],
        out_specs=pl.BlockSpec((tb, N_SEL), lambda i: (i, 0)),
        out_shape=jax.ShapeDtypeStruct((t, N_SEL), F32),
        scratch_shapes=[pltpu.VMEM(u_tab.shape, u_tab.dtype), pltpu.SMEM((2 * tb * N_SEL,), I32),
                        pltpu.VMEM((PEER_CHUNK * N_SEL, V7X_LANES), F32), pltpu.SemaphoreType.DMA((3,))],
        compiler_params=_cparams("arbitrary"),
        name="peer_up",
    )(x3, g, idx, u_tab)
    f3 = pl.pallas_call(
        _peer_down_kernel,
        grid=(nblk,),
        in_specs=[pl.BlockSpec((tb, 1, N_SEL), lambda i: (i, 0, 0)), any_spec, any_spec],
        out_specs=pl.BlockSpec((tb,) + x3.shape[1:], lambda i: (i, 0, 0)),
        out_shape=jax.ShapeDtypeStruct(x3.shape, F32),
        scratch_shapes=[pltpu.VMEM(v_tab.shape, v_tab.dtype), pltpu.SMEM((2 * tb * N_SEL,), I32),
                        pltpu.SemaphoreType.DMA((3,))],
        compiler_params=_cparams("arbitrary"),
        name="peer_down",
    )(w.reshape(t, 1, N_SEL), idx, v_tab)
    return f3.reshape(t, d)


def _final_kernel(base_ref, f_ref, g_ref, b_ref, o_ref):
    o_ref[...] = _layer_norm(base_ref[...] + f_ref[...], g_ref[...], b_ref[...])


def _final(base, f, g2, b2):
    t, d = base.shape
    tm = min(ROW_TILE, t)
    row = pl.BlockSpec((tm, d), lambda i: (i, 0))
    vec = pl.BlockSpec((1, d), lambda i: (0, 0))
    return pl.pallas_call(
        _final_kernel,
        grid=(t // tm,),
        in_specs=[row, row, vec, vec],
        out_specs=row,
        out_shape=jax.ShapeDtypeStruct((t, d), F32),
        compiler_params=_cparams("parallel"),
        name="residual_ln2",
    )(base, f, g2, b2)


def _reorder_in_proj(w, b):
    hk = M_HEADS * M_DK
    widths = (hk, hk, M_HEADS * M_DV, M_HEADS * M_DV, M_HEADS, M_HEADS,
              A_HEADS * 2 * A_DK, A_HEADS * 2 * A_DK, A_HEADS * A_DV)
    cuts = [0]
    for n in widths:
        cuts.append(cuts[-1] + n)
    total = w.shape[1]
    d_model = (total - cuts[-1]) // 2
    cuts += [cuts[-1] + d_model, total]
    piece = lambda a, i: a[..., cuts[i]:cuts[i + 1]]
    order = (0, 1, 2, 3, 6, 7, 8, 9, 10, 4, 5)
    pad = V7X_LANES - 2 * M_HEADS
    wr = jnp.concatenate([piece(w, i) for i in order] + [jnp.zeros((w.shape[0], pad), w.dtype)], axis=1)
    br = jnp.concatenate([piece(b, i) for i in order] + [jnp.zeros((pad,), b.dtype)], axis=0)
    return wr.astype(BF16), br.reshape(1, -1)


def _table_tiles(tab):
    n, d = tab.shape
    return tab.astype(BF16).reshape(n, V7X_SUBLANES, d // V7X_SUBLANES)


def _token_local(x2, ya, yb, sga, sgb, p2, lw, alpha):
    t = x2.shape[0]
    if t % PEER_TILE:
        pad = lambda a: jnp.pad(a, ((0, PEER_TILE - t % PEER_TILE), (0, 0)))
        return _token_local(pad(x2), pad(ya), pad(yb), pad(sga), pad(sgb), pad(p2), lw, alpha)[:t]
    x1, base, q = _merge(x2, ya, yb, sga, sgb, p2, lw["wa"], lw["wb"], lw["wo"], lw["g1"], lw["b1"],
                         lw["wg"], lw["wp"], lw["wq"], alpha)
    idx, g = _route(q, lw["subkeys"], ROUTE_TILE if t % ROUTE_TILE == 0 else PEER_TILE)
    f = _peer_experts(x1, g, idx, lw["u_tab"], lw["v_tab"], PEER_TILE)
    return _final(base, f, lw["g2"], lw["b2"])


def kernel(x_prompt, x_sample, cache_k, cache_v, state_C, state_n, state_m, page_table, p_prompt, p_sample,
           w_in, b_in, lambda_q1, lambda_k1, lambda_q2, lambda_k2, mlstm_norm_g, diff_norm_g, w_branch_a,
           w_branch_b, w_out, ln1_g, ln1_b, peer_wq, peer_subkeys, peer_u, peer_v, ple_w_gate, ple_w_proj,
           ln2_g, ln2_b):
    depth = w_in.shape[0]
    batch, seq, d_model = x_prompt.shape
    db, dseq, _ = x_sample.shape
    assert dseq == 1, "the decode path handles one new token per request"
    alpha = (2.0 * depth) ** 0.25
    slopes = 2.0 ** (-8.0 * (jnp.arange(A_HEADS, dtype=F32) + 1.0) / A_HEADS)
    slopes_h = jnp.broadcast_to(slopes[:, None, None], (A_HEADS, 1, V7X_LANES))
    slopes_8 = jnp.broadcast_to(jnp.repeat(slopes, 2)[:, None], (2 * A_HEADS, V7X_LANES))

    hp = x_prompt.reshape(batch * seq, d_model)
    hs = x_sample.reshape(db, d_model)
    outs = [[] for _ in range(10)]
    for l in range(depth):
        lam_init = 0.8 - 0.6 * math.exp(-0.3 * l)
        w_l, b_l = _reorder_in_proj(w_in[l], b_in[l])
        row = lambda a: a.reshape(1, -1)
        lw = dict(wa=w_branch_a[l].astype(BF16), wb=w_branch_b[l].astype(BF16), wo=w_out[l].astype(BF16),
                  g1=row(ln1_g[l]), b1=row(ln1_b[l]), wg=ple_w_gate[l].astype(BF16),
                  wp=ple_w_proj[l].astype(BF16), wq=peer_wq[l].astype(BF16),
                  subkeys=peer_subkeys[l].astype(BF16), u_tab=_table_tiles(peer_u[l]),
                  v_tab=_table_tiles(peer_v[l]), g2=row(ln2_g[l]), b2=row(ln2_b[l]))
        lams = [row(a[l].astype(F32)) for a in (lambda_q1, lambda_k1, lambda_q2, lambda_k2)]
        mng = row(mlstm_norm_g[l])
        dng = row(diff_norm_g[l])

        mq, mk, mv, so, aq, akt, avf, akb, avb, sga, sgb, gates = _project(hp, w_l, b_l, seq=seq)
        ya, c_p, n_p, m_p = _mlstm_prompt(mq, mk, mv, so, gates, mng, batch, seq)
        yb = _attn_prompt(aq, akb, avb, slopes_h, dng, lams, batch, seq, lam_init)
        hp = _token_local(hp, ya, yb, sga, sgb, p_prompt[l].reshape(batch * seq, -1), lw, alpha)
        outs[0].append(jnp.transpose(akt.reshape(batch, A_HEADS, 2, A_DK, seq), (0, 4, 1, 2, 3)))
        outs[1].append(avf.reshape(batch, seq, A_HEADS, A_DV))
        outs[2].append(c_p)
        outs[3].append(n_p)
        outs[4].append(m_p[:, 0, :M_HEADS])

        mq, mk, mv, so, aq, akf, avf, akb, avb, sga, sgb, gates = _project(hs, w_l, b_l)
        m0 = jnp.pad(state_m[l].astype(F32), ((0, 0), (0, V7X_LANES - M_HEADS))).reshape(db, 1, V7X_LANES)
        ya, c_s, n_s, m_s = _mlstm_step(mq, mk, mv, so, gates, mng, state_C[l].astype(F32),
                                        state_n[l].astype(F32), m0)
        yb = _attn_decode(page_table, aq, akb, avb, cache_k[l], cache_v[l], slopes_8, dng, lams, lam_init)
        hs = _token_local(hs, ya.reshape(db, -1), yb, sga, sgb, p_sample[l].reshape(db, -1), lw, alpha)
        outs[5].append(akf.reshape(db, 1, A_HEADS, 2, A_DK))
        outs[6].append(avf.reshape(db, 1, A_HEADS, A_DV))
        outs[7].append(c_s)
        outs[8].append(n_s)
        outs[9].append(m_s[:, 0, :M_HEADS])

    st = [jnp.stack(o, 0) for o in outs]
    return (hp.reshape(batch, seq, d_model), hs.reshape(db, 1, d_model),
            st[0], st[1], st[2], st[3], st[4], st[5], st[6], st[7], st[8], st[9])
```
